```python
import math
import jax, jax.numpy as jnp
from jax import lax
import numpy as np

D_MODEL = 1024
BATCH = 8
SEQ = 2048
DEPTH = 4

HEAD_DIM = 64
N_A_LAYERS = DEPTH // 2
N_B_LAYERS = DEPTH - N_A_LAYERS
DIFF_HEADS = D_MODEL // (2 * HEAD_DIM)
DIL_GROUPS = ((128, 1), (512, 4), (2048, 16))
N_GROUPS = len(DIL_GROUPS)
DIL_HEADS = D_MODEL // HEAD_DIM
D_FF = 4 * D_MODEL
ROPE_THETA = 10000.0
BLOCK = 128
LAM_STD = 0.1
EPS = 1e-6

kernel_name = 'yoco_diffattn_dilated_hybrid'


def rms_norm(x, gain):
    xf = x.astype(jnp.float32)
    y = xf * lax.rsqrt(jnp.mean(xf * xf, axis=-1, keepdims=True) + EPS)
    return (y * gain.astype(jnp.float32)).astype(x.dtype)


def rope_tables(seq, dim):
    inv = 1.0 / (ROPE_THETA ** (jnp.arange(0, dim, 2, dtype=jnp.float32) / dim))
    ang = jnp.arange(seq, dtype=jnp.float32)[:, None] * inv[None, :]
    return jnp.cos(ang), jnp.sin(ang)


def apply_rope(x, cos, sin):
    d2 = x.shape[-1] // 2
    x1, x2 = x[..., :d2], x[..., d2:]
    c = cos[None, :, None, :].astype(x.dtype)
    s = sin[None, :, None, :].astype(x.dtype)
    return jnp.concatenate([x1 * c - x2 * s, x2 * c + x1 * s], axis=-1)


def squared_relu_mlp(x, gain, w_up, w_down):
    h = rms_norm(x, gain)
    return jnp.square(jax.nn.relu(h @ w_up)) @ w_down


def diff_attention(h, w_qkv, q_gain, k_gain, lam_q1, lam_k1, lam_q2, lam_k2, sub_gain, lam_init, cos, sin):
    B, S, _ = h.shape
    H, D = DIFF_HEADS, HEAD_DIM
    q, k, v = jnp.split(h @ w_qkv, 3, axis=-1)
    q = apply_rope(rms_norm(q.reshape(B, S, 2 * H, D), q_gain), cos, sin).reshape(B, S, H, 2, D)
    k = apply_rope(rms_norm(k.reshape(B, S, 2 * H, D), k_gain), cos, sin).reshape(B, S, H, 2, D)
    v = v.reshape(B, S, H, 2 * D)
    f32 = jnp.float32
    lam = (jnp.exp(jnp.sum(lam_q1.astype(f32) * lam_k1.astype(f32)))
           - jnp.exp(jnp.sum(lam_q2.astype(f32) * lam_k2.astype(f32))) + lam_init)
    scale = D ** -0.5
    outs = []
    for i in range(S // BLOCK):
        n_k = (i + 1) * BLOCK
        q_blk = q[:, i * BLOCK:n_k]
        s = jnp.einsum('bqhcd,bkhcd->bhcqk', q_blk, k[:, :n_k]).astype(f32) * scale
        qpos = i * BLOCK + jnp.arange(BLOCK)
        kpos = jnp.arange(n_k)
        s = jnp.where(kpos[None, :] <= qpos[:, None], s, -jnp.inf)
        p = jax.nn.softmax(s, axis=-1)
        a = p[:, :, 0] - lam * p[:, :, 1]
        outs.append(jnp.einsum('bhqk,bkhe->bqhe', a.astype(v.dtype), v[:, :n_k]))
    o = jnp.concatenate(outs, axis=1)
    o = rms_norm(o, sub_gain) * (1.0 - lam_init)
    return o.reshape(B, S, H * 2 * D)


def dilated_group_attention(q, k, v, window, dilation):
    B, S, H, D = q.shape
    r = dilation
    L = S // r
    span = window // dilation
    nb = -(-L // BLOCK)
    Lp = nb * BLOCK
    Z = B * r

    def to_sub(a):
        a = a.reshape(B, L, r, H, D).transpose(0, 2, 1, 3, 4).reshape(Z, L, H, D)
        return jnp.pad(a, ((0, 0), (0, Lp - L), (0, 0), (0, 0)))

    def with_prev(a):
        a = jnp.pad(a, ((0, 0), (BLOCK, 0), (0, 0), (0, 0))).reshape(Z, nb + 1, BLOCK, H, D)
        return jnp.concatenate([a[:, :-1], a[:, 1:]], axis=2)

    qb = to_sub(q).reshape(Z, nb, BLOCK, H, D)
    kb = with_prev(to_sub(k))
    vb = with_prev(to_sub(v))
    s = jnp.einsum('znqhd,znkhd->znhqk', qb, kb).astype(jnp.float32) * (D ** -0.5)
    blk = jnp.arange(nb)[:, None, None] * BLOCK
    qi = blk + jnp.arange(BLOCK)[None, :, None]
    kj = blk - BLOCK + jnp.arange(2 * BLOCK)[None, None, :]
    dist = qi - kj
    valid = (dist >= 0) & (dist <= span) & (kj >= 0)
    s = jnp.where(valid[None, :, None], s, -jnp.inf)
    m = jnp.max(s, axis=-1, keepdims=True)
    p = jnp.exp(s - m)
    den = jnp.sum(p, axis=-1)
    o = jnp.einsum('znhqk,znkhd->znqhd', p.astype(vb.dtype), vb).astype(jnp.float32)
    o = o / jnp.transpose(den, (0, 1, 3, 2))[..., None]
    lse = jnp.transpose(m[..., 0] + jnp.log(den), (0, 1, 3, 2))
    o = o.reshape(Z, Lp, H, D)[:, :L].reshape(B, r, L, H, D).transpose(0, 2, 1, 3, 4).reshape(B, S, H, D)
    lse = lse.reshape(Z, Lp, H)[:, :L].reshape(B, r, L, H).transpose(0, 2, 1, 3).reshape(B, S, H)
    return o, lse


def shared_kv(x, kv_norm, kv_w, kv_k_gain, cos, sin):
    B, S, _ = x.shape
    G, H, D = N_GROUPS, DIL_HEADS, HEAD_DIM
    k, v = jnp.split(rms_norm(x, kv_norm) @ kv_w, 2, axis=-1)
    k = rms_norm(k.reshape(B, S, G, H, D), kv_k_gain[:, None, :])
    k = apply_rope(k.reshape(B, S, G * H, D), cos, sin).reshape(B, S, G, H, D)
    return k, v.reshape(B, S, G, H, D)


def dilated_mixer(x, norm_gain, w_q, q_gain, w_o, k_sh, v_sh, cos, sin):
    B, S, _ = x.shape
    G, H, D = N_GROUPS, DIL_HEADS, HEAD_DIM
    q = rms_norm(rms_norm(x, norm_gain) @ w_q, jnp.ones((), x.dtype)).reshape(B, S, G, H, D) if False else (rms_norm(x, norm_gain) @ w_q).reshape(B, S, G, H, D)
    q = rms_norm(q, q_gain[:, None, :])
    q = apply_rope(q.reshape(B, S, G * H, D), cos, sin).reshape(B, S, G, H, D)
    outs, lses = [], []
    for g, (window, dilation) in enumerate(DIL_GROUPS):
        o_g, lse_g = dilated_group_attention(q[:, :, g], k_sh[:, :, g], v_sh[:, :, g], window, dilation)
        outs.append(o_g)
        lses.append(lse_g)
    wts = jax.nn.softmax(jnp.stack(lses, axis=0), axis=0)
    o = jnp.sum(wts[..., None] * jnp.stack(outs, axis=0), axis=0)
    return o.astype(x.dtype).reshape(B, S, H * D) @ w_o


def setup_inputs(seed: int = 0) -> dict:
    key = jax.random.key(seed)
    ks = jax.random.split(key, 21)
    f32 = jnp.float32
    nA, nB, G, D = N_A_LAYERS, N_B_LAYERS, N_GROUPS, HEAD_DIM

    def w(k, shape, fan_in):
        return jax.random.normal(k, shape, f32) * (fan_in ** -0.5)

    def gain(k, shape):
        return 1.0 + 0.02 * jax.random.normal(k, shape, f32)

    return {
        'x': jax.random.normal(ks[0], (BATCH, SEQ, D_MODEL), f32),
        'a_norm': gain(ks[1], (nA, D_MODEL)),
        'a_w_qkv': w(ks[2], (nA, D_MODEL, 3 * D_MODEL), D_MODEL),
        'a_q_gain': gain(ks[3], (nA, D)),
        'a_k_gain': gain(ks[4], (nA, D)),
        'a_lam_q1': LAM_STD * jax.random.normal(ks[5], (nA, D), f32),
        'a_lam_k1': LAM_STD * jax.random.normal(ks[6], (nA, D), f32),
        'a_lam_q2': LAM_STD * jax.random.normal(ks[7], (nA, D), f32),
        'a_lam_k2': LAM_STD * jax.random.normal(ks[8], (nA, D), f32),
        'a_sub_gain': gain(ks[9], (nA, 2 * D)),
        'a_w_o': w(ks[10], (nA, D_MODEL, D_MODEL), D_MODEL),
        'kv_norm': gain(ks[11], (D_MODEL,)),
        'kv_w': w(ks[12], (D_MODEL, 2 * G * DIL_HEADS * D), D_MODEL),
        'kv_k_gain': gain(ks[13], (G, D)),
        'b_norm': gain(ks[14], (nB, D_MODEL)),
        'b_w_q': w(ks[15], (nB, D_MODEL, G * DIL_HEADS * D), D_MODEL),
        'b_q_gain': gain(ks[16], (nB, G, D)),
        'b_w_o': w(ks[17], (nB, DIL_HEADS * D, D_MODEL), DIL_HEADS * D),
        'm_norm': gain(ks[18], (DEPTH, D_MODEL)),
        'm_w_up': w(ks[19], (DEPTH, D_MODEL, D_FF), D_MODEL),
        'm_w_down': w(ks[20], (DEPTH, D_FF, D_MODEL), D_FF),
    }


def reference(x, a_norm, a_w_qkv, a_q_gain, a_k_gain, a_lam_q1, a_lam_k1, a_lam_q2, a_lam_k2,
              a_sub_gain, a_w_o, kv_norm, kv_w, kv_k_gain, b_norm, b_w_q, b_q_gain, b_w_o,
              m_norm, m_w_up, m_w_down):
    S = x.shape[1]
    cos, sin = rope_tables(S, HEAD_DIM)
    k_sh, v_sh = None, None
    for layer in range(DEPTH):
        if layer < N_A_LAYERS:
            lam_init = 0.8 - 0.6 * math.exp(-0.3 * layer)
            h = rms_norm(x, a_norm[layer])
            att = diff_attention(h, a_w_qkv[layer], a_q_gain[layer], a_k_gain[layer],
                                 a_lam_q1[layer], a_lam_k1[layer], a_lam_q2[layer], a_lam_k2[layer],
                                 a_sub_gain[layer], lam_init, cos, sin)
            x = x + att @ a_w_o[layer]
        else:
            if layer == N_A_LAYERS:
                k_sh, v_sh = shared_kv(x, kv_norm, kv_w, kv_k_gain, cos, sin)
            bl = layer - N_A_LAYERS
            x = x + dilated_mixer(x, b_norm[bl], b_w_q[bl], b_q_gain[bl], b_w_o[bl], k_sh, v_sh, cos, sin)
        x = x + squared_relu_mlp(x, m_norm[layer], m_w_up[layer], m_w_down[layer])
    return x
```

```python
import functools
import math

import jax
import jax.numpy as jnp
from jax import lax
from jax.experimental import pallas as pl
from jax.experimental.pallas import tpu as pltpu

D_MODEL = 1024
HEAD_DIM = 64
DEPTH = 4
N_A_LAYERS = DEPTH // 2
DIFF_HEADS = D_MODEL // (2 * HEAD_DIM)
DIL_GROUPS = ((128, 1), (512, 4), (2048, 16))
N_GROUPS = len(DIL_GROUPS)
DIL_HEADS = D_MODEL // HEAD_DIM
D_FF = 4 * D_MODEL
ROPE_THETA = 10000.0
BLOCK = 128
EPS = 1e-6

LANES = 128
MXU_TILE = 256
VMEM_LIMIT = 56 * 1024 * 1024
NEG_BIG = -1e30

F32 = jnp.float32
BF16 = jnp.bfloat16


def _params(n_axes):
    return pltpu.CompilerParams(
        dimension_semantics=("arbitrary",) * n_axes,
        vmem_limit_bytes=VMEM_LIMIT)


def _proj_kernel(x_ref, g_ref, w_ref, hg_ref, cos_ref, sa_ref, sb_ref, gm_ref,
                 *out_refs, n_rope, out_widths):
    x = x_ref[...]
    ms = jnp.mean(x * x, axis=-1, keepdims=True)
    h = (x * lax.rsqrt(ms + EPS) * g_ref[...]).astype(BF16)
    cos = cos_ref[...]
    sa = sa_ref[...]
    sb = sb_ref[...]
    col = 0
    for out_ref, width in zip(out_refs, out_widths):
        for c in range(width // MXU_TILE):
            y = jnp.dot(h, w_ref[:, col:col + MXU_TILE],
                        preferred_element_type=F32)
            if col < n_rope:
                ss = jnp.dot((y * y).astype(BF16), gm_ref[...],
                             preferred_element_type=F32)
                yn = y * lax.rsqrt(ss * (1.0 / HEAD_DIM) + EPS) \
                    * hg_ref[:, col:col + MXU_TILE]
                parts = []
                for s in range(MXU_TILE // LANES):
                    z = yn[:, s * LANES:(s + 1) * LANES]
                    z = (z * cos + pltpu.roll(z, LANES - 32, 1) * sa
                         + pltpu.roll(z, 32, 1) * sb)
                    parts.append(z)
                y = jnp.concatenate(parts, axis=1)
            out_ref[:, c * MXU_TILE:(c + 1) * MXU_TILE] = y.astype(out_ref.dtype)
            col += MXU_TILE


def _project(x2, gain, w, head_gain, rope, gmat, out_widths, n_rope, seq, tm=512):
    n = x2.shape[0]
    n_out = w.shape[1]
    cos, sa, sb = rope
    pos_blocks = seq // tm
    kern = functools.partial(_proj_kernel, n_rope=n_rope, out_widths=out_widths)
    rope_spec = pl.BlockSpec((tm, LANES), lambda i: (i % pos_blocks, 0))
    return pl.pallas_call(
        kern,
        grid=(n // tm,),
        in_specs=[
            pl.BlockSpec((tm, D_MODEL), lambda i: (i, 0)),
            pl.BlockSpec((1, D_MODEL), lambda i: (0, 0)),
            pl.BlockSpec((D_MODEL, n_out), lambda i: (0, 0)),
            pl.BlockSpec((1, n_rope), lambda i: (0, 0)),
            rope_spec, rope_spec, rope_spec,
            pl.BlockSpec((MXU_TILE, MXU_TILE), lambda i: (0, 0)),
        ],
        out_specs=[pl.BlockSpec((tm, wd), lambda i: (i, 0)) for wd in out_widths],
        out_shape=[jax.ShapeDtypeStruct((n, wd), BF16) for wd in out_widths],
        compiler_params=_params(1),
        name="proj",
    )(x2, gain.reshape(1, D_MODEL), w, head_gain, cos, sa, sb, gmat)


def _diff_attn_kernel(lq1_ref, lk1_ref, lq2_ref, lk2_ref, sg_ref,
                      q_ref, k_ref, v_ref, o_ref, *, tq, lam_init):
    qi = pl.program_id(2)
    q = q_ref[0]
    lane = lax.broadcasted_iota(jnp.int32, (tq, LANES), 1)
    zero = jnp.zeros_like(q)
    q_c = (jnp.where(lane < HEAD_DIM, q, zero), jnp.where(lane >= HEAD_DIM, q, zero))

    def block(j, carry, mask):
        start = pl.multiple_of(j * tq, tq)
        kb = k_ref[0, pl.ds(start, tq), :]
        vb = v_ref[0, pl.ds(start, tq), :]
        new = []
        for c in range(2):
            m, l, a = carry[3 * c:3 * c + 3]
            s = lax.dot_general(q_c[c], kb, (((1,), (1,)), ((), ())),
                                preferred_element_type=F32)
            if mask is not None:
                s = jnp.where(mask, s, NEG_BIG)
            mn = jnp.maximum(m, jnp.max(s, axis=-1, keepdims=True))
            alpha = jnp.exp(m - mn)
            p = jnp.exp(s - mn)
            l = alpha * l + jnp.sum(p, axis=-1, keepdims=True)
            a = alpha * a + jnp.dot(p.astype(BF16), vb, preferred_element_type=F32)
            new += [mn, l, a]
        return tuple(new)

    m0 = jnp.full((tq, 1), NEG_BIG, F32)
    l0 = jnp.zeros((tq, 1), F32)
    a0 = jnp.zeros((tq, LANES), F32)
    carry = (m0, l0, a0, m0, l0, a0)
    carry = lax.fori_loop(0, qi, lambda j, c: block(j, c, None), carry)
    row = lax.broadcasted_iota(jnp.int32, (tq, tq), 0)
    colm = lax.broadcasted_iota(jnp.int32, (tq, tq), 1)
    carry = block(qi, carry, colm <= row)

    lam = (jnp.exp(jnp.sum(lq1_ref[...] * lk1_ref[...], axis=-1, keepdims=True))
           - jnp.exp(jnp.sum(lq2_ref[...] * lk2_ref[...], axis=-1, keepdims=True))
           + lam_init)
    _, l_0, a_0, _, l_1, a_1 = carry
    o = a_0 * (1.0 / l_0) - lam * (a_1 * (1.0 / l_1))
    ms = jnp.mean(o * o, axis=-1, keepdims=True)
    o = o * lax.rsqrt(ms + EPS) * sg_ref[...] * (1.0 - lam_init)
    o_ref[0] = o.astype(o_ref.dtype)


def _diff_attention(q, k, v, lam_vecs, sub_gain, lam_init, batch, seq, tq=256):
    q3 = q.reshape(batch, seq, D_MODEL)
    k3 = k.reshape(batch, seq, D_MODEL)
    v3 = v.reshape(batch, seq, D_MODEL)
    vec = pl.BlockSpec((1, HEAD_DIM), lambda b, h, i: (0, 0))
    kern = functools.partial(_diff_attn_kernel, tq=tq, lam_init=lam_init)
    out = pl.pallas_call(
        kern,
        grid=(batch, DIFF_HEADS, seq // tq),
        in_specs=[
            vec, vec, vec, vec,
            pl.BlockSpec((1, 2 * HEAD_DIM), lambda b, h, i: (0, 0)),
            pl.BlockSpec((1, tq, LANES), lambda b, h, i: (b, i, h)),
            pl.BlockSpec((1, seq, LANES), lambda b, h, i: (b, 0, h)),
            pl.BlockSpec((1, seq, LANES), lambda b, h, i: (b, 0, h)),
        ],
        out_specs=pl.BlockSpec((1, tq, LANES), lambda b, h, i: (b, i, h)),
        out_shape=jax.ShapeDtypeStruct((batch, seq, D_MODEL), BF16),
        compiler_params=_params(3),
        name="diff_attn",
    )(*[u.reshape(1, HEAD_DIM) for u in lam_vecs], sub_gain.reshape(1, 2 * HEAD_DIM),
      q3, k3, v3)
    return out.reshape(batch * seq, D_MODEL)


def _dilated_kernel(q_ref, k_ref, v_ref, o_ref, lse_ref, *, tl, has_prev):
    li = pl.program_id(2)
    nqb = tl // BLOCK
    nkeys = 2 * BLOCK if has_prev else BLOCK
    lane = lax.broadcasted_iota(jnp.int32, (BLOCK, LANES), 1)
    row = lax.broadcasted_iota(jnp.int32, (BLOCK, nkeys), 0)
    colm = lax.broadcasted_iota(jnp.int32, (BLOCK, nkeys), 1)
    delta = row - colm

    def qblock(qb, _):
        gb = li * nqb + qb
        r0 = pl.multiple_of(qb * BLOCK, BLOCK)
        if has_prev:
            first = jnp.maximum(gb - 1, 0)
            k0 = pl.multiple_of(first * BLOCK, BLOCK)
            dist = delta + (gb - first) * BLOCK
            valid = (dist >= 0) & (dist <= BLOCK)
        else:
            k0 = pl.multiple_of(gb * BLOCK, BLOCK)
            valid = delta >= 0
        valid2 = jnp.concatenate([valid, valid], axis=0)
        lse_tile = jnp.zeros((BLOCK, LANES), F32)
        for hp in range(DIL_HEADS // 2):
            cs = slice(hp * LANES, (hp + 1) * LANES)
            q = q_ref[0, pl.ds(r0, BLOCK), cs]
            kb = k_ref[0, pl.ds(k0, nkeys), cs]
            vb = v_ref[0, pl.ds(k0, nkeys), cs]
            zero = jnp.zeros_like(q)
            q2 = jnp.concatenate([jnp.where(lane < HEAD_DIM, q, zero),
                                  jnp.where(lane >= HEAD_DIM, q, zero)], axis=0)
            s = lax.dot_general(q2, kb, (((1,), (1,)), ((), ())),
                                preferred_element_type=F32)
            s = jnp.where(valid2, s, NEG_BIG)
            m = jnp.max(s, axis=-1, keepdims=True)
            p = jnp.exp(s - m)
            den = jnp.sum(p, axis=-1, keepdims=True)
            pv = jnp.dot(p.astype(BF16), vb, preferred_element_type=F32)
            pv = pv * (1.0 / den)
            o = jnp.where(lane < HEAD_DIM, pv[:BLOCK], pv[BLOCK:])
            o_ref[0, pl.ds(r0, BLOCK), cs] = o.astype(o_ref.dtype)
            lse = m + jnp.log(den)
            lse_tile = jnp.where(lane == 2 * hp, lse[:BLOCK], lse_tile)
            lse_tile = jnp.where(lane == 2 * hp + 1, lse[BLOCK:], lse_tile)
        lse_ref[0, pl.ds(r0, BLOCK), :] = lse_tile
        return 0

    lax.fori_loop(0, nqb, qblock, 0)


def _dilated_group(q_all, k_all, v_all, g, batch, seq):
    _, r = DIL_GROUPS[g]
    sub_len = seq // r
    tl = min(sub_len, 512)
    width = N_GROUPS * D_MODEL
    view = (batch, sub_len, r * width)
    qv, kv, vv = (a.reshape(view) for a in (q_all, k_all, v_all))
    blocks_per_row = width // D_MODEL
    kern = functools.partial(_dilated_kernel, tl=tl, has_prev=sub_len > BLOCK)
    o, lse = pl.pallas_call(
        kern,
        grid=(batch, r, sub_len // tl),
        in_specs=[
            pl.BlockSpec((1, tl, D_MODEL), lambda b, c, i: (b, i, c * blocks_per_row + g)),
            pl.BlockSpec((1, sub_len, D_MODEL), lambda b, c, i: (b, 0, c * blocks_per_row + g)),
            pl.BlockSpec((1, sub_len, D_MODEL), lambda b, c, i: (b, 0, c * blocks_per_row + g)),
        ],
        out_specs=[
            pl.BlockSpec((1, tl, D_MODEL), lambda b, c, i: (b, i, c)),
            pl.BlockSpec((1, tl, LANES), lambda b, c, i: (b, i, c)),
        ],
        out_shape=[
            jax.ShapeDtypeStruct((batch, sub_len, r * D_MODEL), BF16),
            jax.ShapeDtypeStruct((batch, sub_len, r * LANES), F32),
        ],
        compiler_params=_params(3),
        name=f"dilated_g{g}",
    )(qv, kv, vv)
    return o.reshape(batch * seq, D_MODEL), lse.reshape(batch * seq, LANES)


def _oproj_kernel(x_ref, o_ref, w_ref, out_ref):
    out_ref[...] = x_ref[...] + jnp.dot(o_ref[...], w_ref[...],
                                        preferred_element_type=F32)


def _out_proj(x2, o, w, tm=512):
    n = x2.shape[0]
    row = pl.BlockSpec((tm, D_MODEL), lambda i: (i, 0))
    return pl.pallas_call(
        _oproj_kernel,
        grid=(n // tm,),
        in_specs=[row, row, pl.BlockSpec((D_MODEL, D_MODEL), lambda i: (0, 0))],
        out_specs=row,
        out_shape=jax.ShapeDtypeStruct((n, D_MODEL), F32),
        compiler_params=_params(1),
        name="out_proj",
    )(x2, o, w)


def _mix_oproj_kernel(x_ref, o0_ref, o1_ref, o2_ref, l0_ref, l1_ref, l2_ref,
                      e_ref, w_ref, out_ref):
    lses = (l0_ref[...], l1_ref[...], l2_ref[...])
    mx = jnp.maximum(jnp.maximum(lses[0], lses[1]), lses[2])
    ws = [jnp.exp(l - mx) for l in lses]
    inv = 1.0 / (ws[0] + ws[1] + ws[2])
    mixed = None
    for wg, o_ref in zip(ws, (o0_ref, o1_ref, o2_ref)):
        wg = wg * inv
        hi = wg.astype(BF16)
        lo = (wg - hi.astype(F32)).astype(BF16)
        wexp = jnp.dot(jnp.concatenate([hi, lo], axis=1), e_ref[...],
                       preferred_element_type=F32)
        term = wexp * o_ref[...].astype(F32)
        mixed = term if mixed is None else mixed + term
    out_ref[...] = x_ref[...] + jnp.dot(mixed.astype(BF16), w_ref[...],
                                        preferred_element_type=F32)


def _mix_out_proj(x2, outs, lses, expand, w, tm=512):
    n = x2.shape[0]
    row = pl.BlockSpec((tm, D_MODEL), lambda i: (i, 0))
    lrow = pl.BlockSpec((tm, LANES), lambda i: (i, 0))
    return pl.pallas_call(
        _mix_oproj_kernel,
        grid=(n // tm,),
        in_specs=[row, row, row, row, lrow, lrow, lrow,
                  pl.BlockSpec((2 * LANES, D_MODEL), lambda i: (0, 0)),
                  pl.BlockSpec((D_MODEL, D_MODEL), lambda i: (0, 0))],
        out_specs=row,
        out_shape=jax.ShapeDtypeStruct((n, D_MODEL), F32),
        compiler_params=_params(1),
        name="mix_out_proj",
    )(x2, *outs, *lses, expand, w)


def _mlp_kernel(x_ref, g_ref, wu_ref, wd_ref, out_ref, *, ck):
    x = x_ref[...]
    ms = jnp.mean(x * x, axis=-1, keepdims=True)
    h = (x * lax.rsqrt(ms + EPS) * g_ref[...]).astype(BF16)
    out_ref[...] = x
    for c in range(D_FF // ck):
        u = jnp.dot(h, wu_ref[:, c * ck:(c + 1) * ck], preferred_element_type=F32)
        a = jnp.square(jnp.maximum(u, 0.0)).astype(BF16)
        out_ref[...] += jnp.dot(a, wd_ref[c * ck:(c + 1) * ck, :],
                                preferred_element_type=F32)


def _mlp(x2, gain, w_up, w_down, tm=512, ck=512):
    n = x2.shape[0]
    row = pl.BlockSpec((tm, D_MODEL), lambda i: (i, 0))
    return pl.pallas_call(
        functools.partial(_mlp_kernel, ck=ck),
        grid=(n // tm,),
        in_specs=[row,
                  pl.BlockSpec((1, D_MODEL), lambda i: (0, 0)),
                  pl.BlockSpec((D_MODEL, D_FF), lambda i: (0, 0)),
                  pl.BlockSpec((D_FF, D_MODEL), lambda i: (0, 0))],
        out_specs=row,
        out_shape=jax.ShapeDtypeStruct((n, D_MODEL), F32),
        compiler_params=_params(1),
        name="mlp",
    )(x2, gain.reshape(1, D_MODEL), w_up, w_down)


def _rope_tables(seq):
    inv = 1.0 / (ROPE_THETA ** (jnp.arange(0, HEAD_DIM, 2, dtype=F32) / HEAD_DIM))
    ang = jnp.arange(seq, dtype=F32)[:, None] * inv[None, :]
    cos, sin = jnp.cos(ang), jnp.sin(ang)
    zeros = jnp.zeros_like(sin)
    reps = LANES // HEAD_DIM
    cos_t = jnp.tile(jnp.concatenate([cos, cos], axis=1), (1, reps))
    sin_a = jnp.tile(jnp.concatenate([-sin, zeros], axis=1), (1, reps))
    sin_b = jnp.tile(jnp.concatenate([zeros, sin], axis=1), (1, reps))
    return cos_t, sin_a, sin_b


def _head_sum_matrix():
    i = jnp.arange(MXU_TILE)
    return (i[:, None] // HEAD_DIM == i[None, :] // HEAD_DIM).astype(BF16)


def _head_expand_matrix():
    r = jnp.arange(2 * LANES)[:, None] % LANES
    c = jnp.arange(D_MODEL)[None, :] // HEAD_DIM
    return (r == c).astype(BF16)


def kernel(x, a_norm, a_w_qkv, a_q_gain, a_k_gain, a_lam_q1, a_lam_k1, a_lam_q2, a_lam_k2, a_sub_gain, a_w_o, kv_norm, kv_w, kv_k_gain, b_norm, b_w_q, b_q_gain, b_w_o, m_norm, m_w_up, m_w_down):
    batch, seq, _ = x.shape
    n = batch * seq
    scale = HEAD_DIM ** -0.5
    rope = _rope_tables(seq)
    gmat = _head_sum_matrix()
    expand = _head_expand_matrix()
    x2 = x.reshape(n, D_MODEL)
    gw = N_GROUPS * D_MODEL

    k_sh = v_sh = None
    for layer in range(DEPTH):
        if layer < N_A_LAYERS:
            lam_init = 0.8 - 0.6 * math.exp(-0.3 * layer)
            hg = jnp.concatenate([jnp.tile(a_q_gain[layer] * scale, 2 * DIFF_HEADS),
                                  jnp.tile(a_k_gain[layer], 2 * DIFF_HEADS)]).reshape(1, -1)
            q, k, v = _project(x2, a_norm[layer], a_w_qkv[layer].astype(BF16), hg, rope, gmat,
                               (D_MODEL, D_MODEL, D_MODEL), 2 * D_MODEL, seq)
            att = _diff_attention(
                q, k, v,
                (a_lam_q1[layer], a_lam_k1[layer], a_lam_q2[layer], a_lam_k2[layer]),
                a_sub_gain[layer], lam_init, batch, seq)
            x2 = _out_proj(x2, att, a_w_o[layer].astype(BF16))
        else:
            if layer == N_A_LAYERS:
                hg = jnp.repeat(kv_k_gain, DIL_HEADS, axis=0).reshape(1, gw)
                k_sh, v_sh = _project(x2, kv_norm, kv_w.astype(BF16), hg, rope, gmat,
                                      (gw, gw), gw, seq)
                k_sh = k_sh.reshape(batch, seq, gw)
                v_sh = v_sh.reshape(batch, seq, gw)
            bl = layer - N_A_LAYERS
            hg = jnp.repeat(b_q_gain[bl] * scale, DIL_HEADS, axis=0).reshape(1, gw)
            (q,) = _project(x2, b_norm[bl], b_w_q[bl].astype(BF16), hg, rope, gmat,
                            (gw,), gw, seq)
            q = q.reshape(batch, seq, gw)
            outs, lses = [], []
            for g in range(N_GROUPS):
                o_g, lse_g = _dilated_group(q, k_sh, v_sh, g, batch, seq)
                outs.append(o_g)
                lses.append(lse_g)
            x2 = _mix_out_proj(x2, outs, lses, expand, b_w_o[bl].astype(BF16))
        x2 = _mlp(x2, m_norm[layer], m_w_up[layer].astype(BF16),
                  m_w_down[layer].astype(BF16))
    return x2.reshape(batch, seq, D_MODEL)
```

```python
import functools
import math

import jax
import jax.numpy as jnp
from jax import lax
from jax.experimental import pallas as pl
from jax.experimental.pallas import tpu as pltpu

D_MODEL = 1024
HEAD_DIM = 64
HALF_DIM = HEAD_DIM // 2
DEPTH = 4
N_A_LAYERS = DEPTH // 2
DIFF_HEADS = D_MODEL // (2 * HEAD_DIM)
DIL_GROUPS = ((128, 1), (512, 4), (2048, 16))
N_GROUPS = len(DIL_GROUPS)
DIL_HEADS = D_MODEL // HEAD_DIM
D_FF = 4 * D_MODEL
ROPE_THETA = 10000.0
BLOCK = 128
EPS = 1e-6

LANES = 128
MXU_TILE = 256
HEADS_PER_TILE = MXU_TILE // HEAD_DIM
VMEM_LIMIT = 56 * 1024 * 1024
NEG_BIG = -1e30
LOG2_E = math.log2(math.e)

F32 = jnp.float32
BF16 = jnp.bfloat16


def _params(n_axes):
    return pltpu.CompilerParams(
        dimension_semantics=("arbitrary",) * n_axes,
        vmem_limit_bytes=VMEM_LIMIT)


def _head_of_lane(shape, axis):
    lane = lax.broadcasted_iota(jnp.int32, shape, axis)
    return (lane % LANES) // HALF_DIM


def _proj_kernel(x_ref, g_ref, w_ref, hg_ref, cos_ref, sin_ref, gm_ref, *refs,
                 n_rope, dilations, tm):
    out_refs = refs[:len(dilations)]
    slab_scr = refs[len(dilations)]
    x = x_ref[...]
    ms = jnp.mean(x * x, axis=-1, keepdims=True)
    h = (x * lax.rsqrt(ms + EPS) * g_ref[...]).astype(BF16)
    cos = cos_ref[...]
    sin = sin_ref[...]
    col = 0
    n_slab = 0
    chunk = 2 * MXU_TILE
    for out_ref, r in zip(out_refs, dilations):
        for c in range(D_MODEL // chunk):
            y = jnp.dot(h, w_ref[:, col:col + chunk],
                        preferred_element_type=F32)
            slabs = [y[:, s * LANES:(s + 1) * LANES] for s in range(chunk // LANES)]
            if col < n_rope:
                sq = jnp.concatenate([slabs[0] * slabs[0] + slabs[1] * slabs[1],
                                      slabs[2] * slabs[2] + slabs[3] * slabs[3]], axis=1)
                ss = jnp.dot(sq.astype(BF16), gm_ref[...], preferred_element_type=F32)
                rs = lax.rsqrt(ss + EPS)
                rot = []
                for t in range(2):
                    rs_t = rs[:, t * LANES:(t + 1) * LANES]
                    lo = col + t * MXU_TILE
                    z0 = slabs[2 * t] * rs_t * hg_ref[:, lo:lo + LANES]
                    z1 = slabs[2 * t + 1] * rs_t * hg_ref[:, lo + LANES:lo + MXU_TILE]
                    rot += [z0 * cos - z1 * sin, z1 * cos + z0 * sin]
                slabs = rot
            for s, slab_val in enumerate(slabs):
                lo = c * chunk + s * LANES
                if r == 1:
                    out_ref[:, lo:lo + LANES] = slab_val.astype(out_ref.dtype)
                else:
                    slab = n_slab % slab_scr.shape[0]
                    n_slab += 1
                    slab_scr[slab] = slab_val
                    for cls in range(r):
                        out_ref[:, cls * D_MODEL + lo:cls * D_MODEL + lo + LANES] = (
                            slab_scr[slab, pl.ds(cls, tm // r, stride=r), :]
                            .astype(out_ref.dtype))
            col += chunk


def _project(x2, gain, w, head_gain, rope, gmat, dilations, n_rope, seq, tm=512):
    n = x2.shape[0]
    n_out = w.shape[1]
    cos, sin = rope
    pos_blocks = seq // tm
    kern = functools.partial(_proj_kernel, n_rope=n_rope, dilations=dilations, tm=tm)
    rope_spec = pl.BlockSpec((tm, LANES), lambda i: (i % pos_blocks, 0))
    return pl.pallas_call(
        kern,
        grid=(n // tm,),
        in_specs=[
            pl.BlockSpec((tm, D_MODEL), lambda i: (i, 0)),
            pl.BlockSpec((1, D_MODEL), lambda i: (0, 0)),
            pl.BlockSpec((D_MODEL, n_out), lambda i: (0, 0)),
            pl.BlockSpec((1, n_rope), lambda i: (0, 0)),
            rope_spec, rope_spec,
            pl.BlockSpec((MXU_TILE, MXU_TILE), lambda i: (0, 0)),
        ],
        out_specs=[pl.BlockSpec((tm // r, r * D_MODEL), lambda i: (i, 0))
                   for r in dilations],
        out_shape=[jax.ShapeDtypeStruct((n // r, r * D_MODEL), BF16) for r in dilations],
        scratch_shapes=[pltpu.VMEM((8, tm, LANES), F32)],
        compiler_params=_params(1),
        name="proj",
    )(x2, gain.reshape(1, D_MODEL), w, head_gain, cos, sin, gmat)


def _diff_attn_kernel(lq1_ref, lk1_ref, lq2_ref, lk2_ref, sg_ref,
                      q_ref, k_ref, v_ref, o_ref, s_scr, p_scr, linv_scr,
                      *, tq, seq, lam_init, row_chunk):
    lam = (jnp.exp(jnp.sum(lq1_ref[...] * lk1_ref[...], axis=-1, keepdims=True))
           - jnp.exp(jnp.sum(lq2_ref[...] * lk2_ref[...], axis=-1, keepdims=True))
           + lam_init)
    first = 2 * (pl.program_id(1) % 2)
    owner = _head_of_lane((tq, MXU_TILE), 1)
    rr = lax.broadcasted_iota(jnp.int32, (row_chunk, LANES), 0)
    cc = lax.broadcasted_iota(jnp.int32, (row_chunk, LANES), 1)
    delta = rr - cc

    for qi in range(seq // tq):
        buf = qi % 2
        kc = (qi + 1) * tq
        q = q_ref[0, qi * tq:(qi + 1) * tq, :]
        zero = jnp.zeros_like(q)
        q2 = jnp.concatenate([jnp.where(owner == first, q, zero),
                              jnp.where(owner == first + 1, q, zero)], axis=0)
        s_scr[buf, :, :kc] = lax.dot_general(
            q2, k_ref[0, :kc, :], (((1,), (1,)), ((), ())),
            preferred_element_type=F32)

        for r in range(2 * tq // row_chunk):
            rows = slice(r * row_chunk, (r + 1) * row_chunk)
            row0 = (r * row_chunk) % tq

            def tile(c):
                t = s_scr[buf, rows, c * LANES:(c + 1) * LANES]
                col0 = c * LANES - qi * tq
                if col0 + LANES - 1 > row0:
                    t = jnp.where(delta >= col0 - row0, t, NEG_BIG)
                return t

            n_tiles = kc // LANES
            m = tile(0)
            for c in range(1, n_tiles):
                m = jnp.maximum(m, tile(c))
            mrow = jnp.max(m, axis=-1, keepdims=True)
            lsum = None
            for c in range(n_tiles):
                p = jnp.exp2(tile(c) - mrow)
                lsum = p if lsum is None else lsum + p
                p_scr[buf, rows, c * LANES:(c + 1) * LANES] = p.astype(BF16)
            l = jnp.sum(lsum, axis=-1, keepdims=True)
            linv_scr[buf, rows, :] = jnp.broadcast_to(1.0 / l, (row_chunk, LANES))

        acc = jnp.dot(p_scr[buf, :, :kc], v_ref[0, :kc, :],
                      preferred_element_type=F32)
        acc = acc * linv_scr[buf]
        o = acc[:tq] - lam * acc[tq:]
        ms = jnp.mean(o * o, axis=-1, keepdims=True)
        o = o * lax.rsqrt(ms + EPS) * sg_ref[...] * (1.0 - lam_init)
        o_ref[0, qi * tq:(qi + 1) * tq, :] = o.astype(o_ref.dtype)


def _diff_attention(q, k, v, lam_vecs, sub_gain, lam_init, batch, seq, tq=256, row_chunk=64):
    q3 = q.reshape(batch, seq, D_MODEL)
    k3 = k.reshape(batch, seq, D_MODEL)
    v3 = v.reshape(batch, seq, D_MODEL)
    vec = pl.BlockSpec((1, HEAD_DIM), lambda b, h: (0, 0))
    qk_spec = pl.BlockSpec((1, seq, MXU_TILE), lambda b, h: (b, 0, h // 2))
    head = pl.BlockSpec((1, seq, LANES), lambda b, h: (b, 0, h))
    kern = functools.partial(_diff_attn_kernel, tq=tq, seq=seq, lam_init=lam_init,
                             row_chunk=row_chunk)
    out = pl.pallas_call(
        kern,
        grid=(batch, DIFF_HEADS),
        in_specs=[vec, vec, vec, vec,
                  pl.BlockSpec((1, 2 * HEAD_DIM), lambda b, h: (0, 0)),
                  qk_spec, qk_spec, head],
        out_specs=head,
        out_shape=jax.ShapeDtypeStruct((batch, seq, D_MODEL), BF16),
        scratch_shapes=[pltpu.VMEM((2, 2 * tq, seq), F32),
                        pltpu.VMEM((2, 2 * tq, seq), BF16),
                        pltpu.VMEM((2, 2 * tq, LANES), F32)],
        compiler_params=_params(2),
        name="diff_attn",
    )(*[u.reshape(1, HEAD_DIM) for u in lam_vecs], sub_gain.reshape(1, 2 * HEAD_DIM),
      q3, k3, v3)
    return out.reshape(batch * seq, D_MODEL)


def _dilated_kernel(q_ref, k_ref, v_ref, o_ref, lse_ref, *, tl, has_prev):
    li = pl.program_id(2)
    nqb = tl // BLOCK
    nkeys = 2 * BLOCK if has_prev else BLOCK
    owner = _head_of_lane((BLOCK, MXU_TILE), 1)
    vhead = lax.broadcasted_iota(jnp.int32, (BLOCK, MXU_TILE), 1) // HEAD_DIM
    lane = lax.broadcasted_iota(jnp.int32, (BLOCK, LANES), 1)
    row = lax.broadcasted_iota(jnp.int32, (BLOCK, nkeys), 0)
    colm = lax.broadcasted_iota(jnp.int32, (BLOCK, nkeys), 1)
    delta = row - colm

    def qblock(qb, _):
        gb = li * nqb + qb
        r0 = pl.multiple_of(qb * BLOCK, BLOCK)
        if has_prev:
            first = jnp.maximum(gb - 1, 0)
            k0 = pl.multiple_of(first * BLOCK, BLOCK)
            dist = delta + (gb - first) * BLOCK
            valid = (dist >= 0) & (dist <= BLOCK)
        else:
            k0 = pl.multiple_of(gb * BLOCK, BLOCK)
            valid = delta >= 0
        valid4 = jnp.concatenate([valid] * HEADS_PER_TILE, axis=0)
        lse_tile = jnp.zeros((BLOCK, LANES), F32)
        for t in range(D_MODEL // MXU_TILE):
            cs = slice(t * MXU_TILE, (t + 1) * MXU_TILE)
            q = q_ref[0, pl.ds(r0, BLOCK), cs]
            kb = k_ref[0, pl.ds(k0, nkeys), cs]
            vb = v_ref[0, pl.ds(k0, nkeys), cs]
            zero = jnp.zeros_like(q)
            q4 = jnp.concatenate([jnp.where(owner == j, q, zero)
                                  for j in range(HEADS_PER_TILE)], axis=0)
            s = lax.dot_general(q4, kb, (((1,), (1,)), ((), ())),
                                preferred_element_type=F32)
            s = jnp.where(valid4, s, NEG_BIG)
            m = jnp.max(s, axis=-1, keepdims=True)
            p = jnp.exp(s - m)
            den = jnp.sum(p, axis=-1, keepdims=True)
            pv = jnp.dot(p.astype(BF16), vb, preferred_element_type=F32)
            pv = pv * (1.0 / den)
            lse = m + jnp.log(den)
            o = pv[:BLOCK]
            for j in range(HEADS_PER_TILE):
                rows = slice(j * BLOCK, (j + 1) * BLOCK)
                if j:
                    o = jnp.where(vhead == j, pv[rows], o)
                lse_tile = jnp.where(lane == HEADS_PER_TILE * t + j, lse[rows], lse_tile)
            o_ref[0, pl.ds(r0, BLOCK), cs] = o.astype(o_ref.dtype)
        lse_ref[0, pl.ds(r0, BLOCK), :] = lse_tile
        return 0

    lax.fori_loop(0, nqb, qblock, 0)


def _dilated_group(q_g, k_g, v_g, g, batch, seq):
    _, r = DIL_GROUPS[g]
    sub_len = seq // r
    tl = min(sub_len, 512)
    view = (batch, sub_len, r * D_MODEL)
    kern = functools.partial(_dilated_kernel, tl=tl, has_prev=sub_len > BLOCK)
    o, lse = pl.pallas_call(
        kern,
        grid=(batch, r, sub_len // tl),
        in_specs=[
            pl.BlockSpec((1, tl, D_MODEL), lambda b, c, i: (b, i, c)),
            pl.BlockSpec((1, sub_len, D_MODEL), lambda b, c, i: (b, 0, c)),
            pl.BlockSpec((1, sub_len, D_MODEL), lambda b, c, i: (b, 0, c)),
        ],
        out_specs=[
            pl.BlockSpec((1, tl, D_MODEL), lambda b, c, i: (b, i, c)),
            pl.BlockSpec((1, tl, LANES), lambda b, c, i: (b, i, c)),
        ],
        out_shape=[
            jax.ShapeDtypeStruct((batch, sub_len, r * D_MODEL), BF16),
            jax.ShapeDtypeStruct((batch, sub_len, r * LANES), F32),
        ],
        compiler_params=_params(3),
        name=f"dilated_g{g}",
    )(q_g.reshape(view), k_g.reshape(view), v_g.reshape(view))
    n = batch * seq
    return o.reshape(n // r, r * D_MODEL), lse.reshape(n // r, r * LANES)


def _oproj_kernel(x_ref, o_ref, w_ref, out_ref):
    out_ref[...] = x_ref[...] + jnp.dot(o_ref[...], w_ref[...],
                                        preferred_element_type=F32)


def _out_proj(x2, o, w, tm=512):
    n = x2.shape[0]
    row = pl.BlockSpec((tm, D_MODEL), lambda i: (i, 0))
    return pl.pallas_call(
        _oproj_kernel,
        grid=(n // tm,),
        in_specs=[row, row, pl.BlockSpec((D_MODEL, D_MODEL), lambda i: (0, 0))],
        out_specs=row,
        out_shape=jax.ShapeDtypeStruct((n, D_MODEL), F32),
        compiler_params=_params(1),
        name="out_proj",
    )(x2, o, w)


def _mix_oproj_kernel(x_ref, o0_ref, o1_ref, o2_ref, l0_ref, l1_ref, l2_ref,
                      e_ref, w_ref, out_ref, o_scr, l_scr, *, tm):
    n_slabs = D_MODEL // LANES
    for g, (o_ref, l_ref) in enumerate(((o1_ref, l1_ref), (o2_ref, l2_ref))):
        r = DIL_GROUPS[g + 1][1]
        for cls in range(r):
            dst = pl.ds(cls, tm // r, stride=r)
            l_scr[g, dst, :] = l_ref[:, cls * LANES:(cls + 1) * LANES]
            for s in range(n_slabs):
                lo = cls * D_MODEL + s * LANES
                o_scr[g, s, dst, :] = o_ref[:, lo:lo + LANES].astype(F32)

    lses = (l0_ref[...], l_scr[0], l_scr[1])
    mx = jnp.maximum(jnp.maximum(lses[0], lses[1]), lses[2])
    ws = [jnp.exp(l - mx) for l in lses]
    inv = 1.0 / (ws[0] + ws[1] + ws[2])
    wexp = []
    for wg in ws:
        wg = wg * inv
        hi = wg.astype(BF16)
        lo = (wg - hi.astype(F32)).astype(BF16)
        wexp.append(jnp.dot(jnp.concatenate([hi, lo], axis=1), e_ref[...],
                            preferred_element_type=F32))
    slabs = []
    for s in range(n_slabs):
        cs = slice(s * LANES, (s + 1) * LANES)
        mixed = (wexp[0][:, cs] * o0_ref[:, cs].astype(F32)
                 + wexp[1][:, cs] * o_scr[0, s] + wexp[2][:, cs] * o_scr[1, s])
        slabs.append(mixed.astype(BF16))
    out_ref[...] = x_ref[...] + jnp.dot(jnp.concatenate(slabs, axis=1), w_ref[...],
                                        preferred_element_type=F32)


def _mix_out_proj(x2, outs, lses, expand, w, tm=512):
    n = x2.shape[0]
    row = pl.BlockSpec((tm, D_MODEL), lambda i: (i, 0))
    o_specs = [pl.BlockSpec((tm // r, r * D_MODEL), lambda i: (i, 0)) for _, r in DIL_GROUPS]
    l_specs = [pl.BlockSpec((tm // r, r * LANES), lambda i: (i, 0)) for _, r in DIL_GROUPS]
    return pl.pallas_call(
        functools.partial(_mix_oproj_kernel, tm=tm),
        grid=(n // tm,),
        in_specs=[row, *o_specs, *l_specs,
                  pl.BlockSpec((2 * LANES, D_MODEL), lambda i: (0, 0)),
                  pl.BlockSpec((D_MODEL, D_MODEL), lambda i: (0, 0))],
        out_specs=row,
        out_shape=jax.ShapeDtypeStruct((n, D_MODEL), F32),
        scratch_shapes=[pltpu.VMEM((N_GROUPS - 1, D_MODEL // LANES, tm, LANES), F32),
                        pltpu.VMEM((N_GROUPS - 1, tm, LANES), F32)],
        compiler_params=_params(1),
        name="mix_out_proj",
    )(x2, *outs, *lses, expand, w)


def _mlp_kernel(x_ref, g_ref, wu_ref, wd_ref, out_ref, *, ck):
    x = x_ref[...]
    ms = jnp.mean(x * x, axis=-1, keepdims=True)
    h = (x * lax.rsqrt(ms + EPS) * g_ref[...]).astype(BF16)
    out_ref[...] = x
    for c in range(D_FF // ck):
        u = jnp.dot(h, wu_ref[:, c * ck:(c + 1) * ck], preferred_element_type=F32)
        a = jnp.square(jnp.maximum(u, 0.0)).astype(BF16)
        out_ref[...] += jnp.dot(a, wd_ref[c * ck:(c + 1) * ck, :],
                                preferred_element_type=F32)


def _mlp(x2, gain, w_up, w_down, tm=512, ck=512):
    n = x2.shape[0]
    row = pl.BlockSpec((tm, D_MODEL), lambda i: (i, 0))
    return pl.pallas_call(
        functools.partial(_mlp_kernel, ck=ck),
        grid=(n // tm,),
        in_specs=[row,
                  pl.BlockSpec((1, D_MODEL), lambda i: (0, 0)),
                  pl.BlockSpec((D_MODEL, D_FF), lambda i: (0, 0)),
                  pl.BlockSpec((D_FF, D_MODEL), lambda i: (0, 0))],
        out_specs=row,
        out_shape=jax.ShapeDtypeStruct((n, D_MODEL), F32),
        compiler_params=_params(1),
        name="mlp",
    )(x2, gain.reshape(1, D_MODEL), w_up, w_down)


def _rope_tables(seq):
    inv = 1.0 / (ROPE_THETA ** (jnp.arange(0, HEAD_DIM, 2, dtype=F32) / HEAD_DIM))
    ang = jnp.arange(seq, dtype=F32)[:, None] * inv[None, :]
    reps = LANES // HALF_DIM
    return jnp.tile(jnp.cos(ang), (1, reps)), jnp.tile(jnp.sin(ang), (1, reps))


def _to_rotary_layout(a, n_rope):
    lead = a.shape[:-1]
    rot = a[..., :n_rope].reshape(*lead, n_rope // MXU_TILE, HEADS_PER_TILE, 2, HALF_DIM)
    rot = jnp.swapaxes(rot, -2, -3).reshape(*lead, n_rope)
    return jnp.concatenate([rot, a[..., n_rope:]], axis=-1)


def _head_mean_matrix():
    i = jnp.arange(MXU_TILE)
    same = i[:, None] // HALF_DIM == i[None, :] // HALF_DIM
    return (same.astype(F32) / HEAD_DIM).astype(BF16)


def _head_expand_matrix():
    r = jnp.arange(2 * LANES)[:, None] % LANES
    c = jnp.arange(D_MODEL)[None, :] // HEAD_DIM
    return (r == c).astype(BF16)


def _rotary_weights(w, head_gain, n_rope):
    return (_to_rotary_layout(w, n_rope).astype(BF16),
            _to_rotary_layout(head_gain, n_rope).reshape(1, n_rope))


def kernel(x, a_norm, a_w_qkv, a_q_gain, a_k_gain, a_lam_q1, a_lam_k1, a_lam_q2, a_lam_k2, a_sub_gain, a_w_o, kv_norm, kv_w, kv_k_gain, b_norm, b_w_q, b_q_gain, b_w_o, m_norm, m_w_up, m_w_down):
    batch, seq, _ = x.shape
    n = batch * seq
    scale = HEAD_DIM ** -0.5
    rope = _rope_tables(seq)
    gmat = _head_mean_matrix()
    expand = _head_expand_matrix()
    x2 = x.reshape(n, D_MODEL)
    gw = N_GROUPS * D_MODEL
    dil = tuple(r for _, r in DIL_GROUPS)

    k_sh = v_sh = None
    for layer in range(DEPTH):
        if layer < N_A_LAYERS:
            lam_init = 0.8 - 0.6 * math.exp(-0.3 * layer)
            hg = jnp.concatenate([jnp.tile(a_q_gain[layer] * (scale * LOG2_E), 2 * DIFF_HEADS),
                                  jnp.tile(a_k_gain[layer], 2 * DIFF_HEADS)])
            w, hg = _rotary_weights(a_w_qkv[layer], hg, 2 * D_MODEL)
            q, k, v = _project(x2, a_norm[layer], w, hg, rope, gmat, (1, 1, 1),
                               2 * D_MODEL, seq)
            att = _diff_attention(
                q, k, v,
                (a_lam_q1[layer], a_lam_k1[layer], a_lam_q2[layer], a_lam_k2[layer]),
                a_sub_gain[layer], lam_init, batch, seq)
            x2 = _out_proj(x2, att, a_w_o[layer].astype(BF16))
        else:
            if layer == N_A_LAYERS:
                hg = jnp.repeat(kv_k_gain, DIL_HEADS, axis=0).reshape(gw)
                w, hg = _rotary_weights(kv_w, hg, gw)
                kv = _project(x2, kv_norm, w, hg, rope, gmat, dil + dil, gw, seq)
                k_sh, v_sh = kv[:N_GROUPS], kv[N_GROUPS:]
            bl = layer - N_A_LAYERS
            hg = jnp.repeat(b_q_gain[bl] * scale, DIL_HEADS, axis=0).reshape(gw)
            w, hg = _rotary_weights(b_w_q[bl], hg, gw)
            qs = _project(x2, b_norm[bl], w, hg, rope, gmat, dil, gw, seq)
            outs, lses = [], []
            for g in range(N_GROUPS):
                o_g, lse_g = _dilated_group(qs[g], k_sh[g], v_sh[g], g, batch, seq)
                outs.append(o_g)
                lses.append(lse_g)
            x2 = _mix_out_proj(x2, outs, lses, expand, b_w_o[bl].astype(BF16))
        x2 = _mlp(x2, m_norm[layer], m_w_up[layer].astype(BF16),
                  m_w_down[layer].astype(BF16))
    return x2.reshape(batch, seq, D_MODEL)
```

```python
import functools
import math

import jax
import jax.numpy as jnp
from jax import lax
from jax.experimental import pallas as pl
from jax.experimental.pallas import tpu as pltpu

D_MODEL = 1024
HEAD_DIM = 64
HALF_DIM = HEAD_DIM // 2
DEPTH = 4
N_A_LAYERS = DEPTH // 2
DIFF_HEADS = D_MODEL // (2 * HEAD_DIM)
DIL_GROUPS = ((128, 1), (512, 4), (2048, 16))
N_GROUPS = len(DIL_GROUPS)
DIL_HEADS = D_MODEL // HEAD_DIM
D_FF = 4 * D_MODEL
ROPE_THETA = 10000.0
BLOCK = 128
EPS = 1e-6

LANES = 128
MXU_TILE = 256
HEADS_PER_TILE = MXU_TILE // HEAD_DIM
VMEM_LIMIT = 56 * 1024 * 1024
NEG_BIG = -1e30
LOG2_E = math.log2(math.e)

F32 = jnp.float32
BF16 = jnp.bfloat16


def _params(n_axes):
    return pltpu.CompilerParams(
        dimension_semantics=("arbitrary",) * n_axes,
        vmem_limit_bytes=VMEM_LIMIT)


def _head_of_lane(shape, axis):
    lane = lax.broadcasted_iota(jnp.int32, shape, axis)
    return (lane % LANES) // HALF_DIM


def _proj_kernel(x_ref, g_ref, w_ref, hg_ref, cos_ref, sin_ref, gm_ref, *refs,
                 n_rope, dilations, tm):
    out_refs = refs[:len(dilations)]
    slab_scr = refs[len(dilations)]
    x = x_ref[...]
    ms = jnp.mean(x * x, axis=-1, keepdims=True)
    h = (x * lax.rsqrt(ms + EPS) * g_ref[...]).astype(BF16)
    cos = cos_ref[...]
    sin = sin_ref[...]
    col = 0
    n_slab = 0
    chunk = 2 * MXU_TILE
    for out_ref, r in zip(out_refs, dilations):
        for c in range(D_MODEL // chunk):
            y = jnp.dot(h, w_ref[:, col:col + chunk],
                        preferred_element_type=F32)
            slabs = [y[:, s * LANES:(s + 1) * LANES] for s in range(chunk // LANES)]
            if col < n_rope:
                sq = jnp.concatenate([slabs[0] * slabs[0] + slabs[1] * slabs[1],
                                      slabs[2] * slabs[2] + slabs[3] * slabs[3]], axis=1)
                ss = jnp.dot(sq.astype(BF16), gm_ref[...], preferred_element_type=F32)
                rs = lax.rsqrt(ss + EPS)
                rot = []
                for t in range(2):
                    rs_t = rs[:, t * LANES:(t + 1) * LANES]
                    lo = col + t * MXU_TILE
                    z0 = slabs[2 * t] * rs_t * hg_ref[:, lo:lo + LANES]
                    z1 = slabs[2 * t + 1] * rs_t * hg_ref[:, lo + LANES:lo + MXU_TILE]
                    rot += [z0 * cos - z1 * sin, z1 * cos + z0 * sin]
                slabs = rot
            for s, slab_val in enumerate(slabs):
                lo = c * chunk + s * LANES
                if r == 1:
                    out_ref[:, lo:lo + LANES] = slab_val.astype(out_ref.dtype)
                else:
                    slab = n_slab % slab_scr.shape[0]
                    n_slab += 1
                    slab_scr[slab] = slab_val
                    for cls in range(r):
                        out_ref[:, cls * D_MODEL + lo:cls * D_MODEL + lo + LANES] = (
                            slab_scr[slab, pl.ds(cls, tm // r, stride=r), :]
                            .astype(out_ref.dtype))
            col += chunk


def _project(x2, gain, w, head_gain, rope, gmat, dilations, n_rope, seq, tm=512):
    n = x2.shape[0]
    n_out = w.shape[1]
    cos, sin = rope
    pos_blocks = seq // tm
    kern = functools.partial(_proj_kernel, n_rope=n_rope, dilations=dilations, tm=tm)
    rope_spec = pl.BlockSpec((tm, LANES), lambda i: (i % pos_blocks, 0))
    return pl.pallas_call(
        kern,
        grid=(n // tm,),
        in_specs=[
            pl.BlockSpec((tm, D_MODEL), lambda i: (i, 0)),
            pl.BlockSpec((1, D_MODEL), lambda i: (0, 0)),
            pl.BlockSpec((D_MODEL, n_out), lambda i: (0, 0)),
            pl.BlockSpec((1, n_rope), lambda i: (0, 0)),
            rope_spec, rope_spec,
            pl.BlockSpec((MXU_TILE, MXU_TILE), lambda i: (0, 0)),
        ],
        out_specs=[pl.BlockSpec((tm // r, r * D_MODEL), lambda i: (i, 0))
                   for r in dilations],
        out_shape=[jax.ShapeDtypeStruct((n // r, r * D_MODEL), BF16) for r in dilations],
        scratch_shapes=[pltpu.VMEM((8, tm, LANES), F32)],
        compiler_params=_params(1),
        name="proj",
    )(x2, gain.reshape(1, D_MODEL), w, head_gain, cos, sin, gmat)


def _diff_attn_kernel(lq1_ref, lk1_ref, lq2_ref, lk2_ref, sg_ref,
                      q_ref, k_ref, v_ref, o_ref, s_scr, p_scr, linv_scr,
                      *, tq, seq, lam_init, row_chunk):
    lam = (jnp.exp(jnp.sum(lq1_ref[...] * lk1_ref[...], axis=-1, keepdims=True))
           - jnp.exp(jnp.sum(lq2_ref[...] * lk2_ref[...], axis=-1, keepdims=True))
           + lam_init)
    first = 2 * (pl.program_id(1) % 2)
    owner = _head_of_lane((tq, MXU_TILE), 1)
    rr = lax.broadcasted_iota(jnp.int32, (row_chunk, LANES), 0)
    cc = lax.broadcasted_iota(jnp.int32, (row_chunk, LANES), 1)
    delta = rr - cc

    for qi in range(seq // tq):
        buf = qi % 2
        kc = (qi + 1) * tq
        q = q_ref[0, qi * tq:(qi + 1) * tq, :]
        zero = jnp.zeros_like(q)
        q2 = jnp.concatenate([jnp.where(owner == first, q, zero),
                              jnp.where(owner == first + 1, q, zero)], axis=0)
        s_scr[buf, :, :kc] = lax.dot_general(
            q2, k_ref[0, :kc, :], (((1,), (1,)), ((), ())),
            preferred_element_type=F32)

        for r in range(2 * tq // row_chunk):
            rows = slice(r * row_chunk, (r + 1) * row_chunk)
            row0 = (r * row_chunk) % tq

            def tile(c):
                t = s_scr[buf, rows, c * LANES:(c + 1) * LANES]
                col0 = c * LANES - qi * tq
                if col0 + LANES - 1 > row0:
                    t = jnp.where(delta >= col0 - row0, t, NEG_BIG)
                return t

            n_tiles = kc // LANES
            m = tile(0)
            for c in range(1, n_tiles):
                m = jnp.maximum(m, tile(c))
            mrow = jnp.max(m, axis=-1, keepdims=True)
            lsum = None
            for c in range(n_tiles):
                p = jnp.exp2(tile(c) - mrow)
                lsum = p if lsum is None else lsum + p
                p_scr[buf, rows, c * LANES:(c + 1) * LANES] = p.astype(BF16)
            l = jnp.sum(lsum, axis=-1, keepdims=True)
            linv_scr[buf, rows, :] = jnp.broadcast_to(1.0 / l, (row_chunk, LANES))

        acc = jnp.dot(p_scr[buf, :, :kc], v_ref[0, :kc, :],
                      preferred_element_type=F32)
        acc = acc * linv_scr[buf]
        o = acc[:tq] - lam * acc[tq:]
        ms = jnp.mean(o * o, axis=-1, keepdims=True)
        o = o * lax.rsqrt(ms + EPS) * sg_ref[...] * (1.0 - lam_init)
        o_ref[0, qi * tq:(qi + 1) * tq, :] = o.astype(o_ref.dtype)


def _diff_attention(q, k, v, lam_vecs, sub_gain, lam_init, batch, seq, tq=256, row_chunk=64):
    q3 = q.reshape(batch, seq, D_MODEL)
    k3 = k.reshape(batch, seq, D_MODEL)
    v3 = v.reshape(batch, seq, D_MODEL)
    vec = pl.BlockSpec((1, HEAD_DIM), lambda b, h: (0, 0))
    qk_spec = pl.BlockSpec((1, seq, MXU_TILE), lambda b, h: (b, 0, h // 2))
    head = pl.BlockSpec((1, seq, LANES), lambda b, h: (b, 0, h))
    kern = functools.partial(_diff_attn_kernel, tq=tq, seq=seq, lam_init=lam_init,
                             row_chunk=row_chunk)
    out = pl.pallas_call(
        kern,
        grid=(batch, DIFF_HEADS),
        in_specs=[vec, vec, vec, vec,
                  pl.BlockSpec((1, 2 * HEAD_DIM), lambda b, h: (0, 0)),
                  qk_spec, qk_spec, head],
        out_specs=head,
        out_shape=jax.ShapeDtypeStruct((batch, seq, D_MODEL), BF16),
        scratch_shapes=[pltpu.VMEM((2, 2 * tq, seq), F32),
                        pltpu.VMEM((2, 2 * tq, seq), BF16),
                        pltpu.VMEM((2, 2 * tq, LANES), F32)],
        compiler_params=_params(2),
        name="diff_attn",
    )(*[u.reshape(1, HEAD_DIM) for u in lam_vecs], sub_gain.reshape(1, 2 * HEAD_DIM),
      q3, k3, v3)
    return out.reshape(batch * seq, D_MODEL)


def _dilated_kernel(q_ref, k_ref, v_ref, o_ref, max_ref, den_ref, *, tl, has_prev, classes):
    li = pl.program_id(2)
    nqb = tl // BLOCK
    nkeys = 2 * BLOCK if has_prev else BLOCK
    owner = _head_of_lane((BLOCK, MXU_TILE), 1)
    vhead = lax.broadcasted_iota(jnp.int32, (BLOCK, MXU_TILE), 1) // HEAD_DIM
    row = lax.broadcasted_iota(jnp.int32, (BLOCK, nkeys), 0)
    colm = lax.broadcasted_iota(jnp.int32, (BLOCK, nkeys), 1)
    delta = row - colm

    for qb in range(nqb):
        gb = li * nqb + qb
        rows_q = slice(qb * BLOCK, (qb + 1) * BLOCK)
        if has_prev:
            first = jnp.maximum(gb - 1, 0)
            k0 = pl.multiple_of(first * BLOCK, BLOCK)
            dist = delta + (gb - first) * BLOCK
            valid = (dist >= 0) & (dist <= BLOCK)
        else:
            k0 = pl.multiple_of(gb * BLOCK, BLOCK)
            valid = delta >= 0
        bias = jnp.where(valid, 0.0, NEG_BIG)
        for cl in range(classes):
            stat_cols = slice(cl * LANES, (cl + 1) * LANES)
            max_ref[0, rows_q, stat_cols] = jnp.zeros((BLOCK, LANES), F32)
            den_ref[0, rows_q, stat_cols] = jnp.ones((BLOCK, LANES), F32)
            for t in range(D_MODEL // MXU_TILE):
                lo = cl * D_MODEL + t * MXU_TILE
                cs = slice(lo, lo + MXU_TILE)
                q = q_ref[0, rows_q, cs]
                kb = k_ref[0, pl.ds(k0, nkeys), cs]
                vb = v_ref[0, pl.ds(k0, nkeys), cs]
                zero = jnp.zeros_like(q)
                q4 = jnp.concatenate([jnp.where(owner == j, q, zero)
                                      for j in range(HEADS_PER_TILE)], axis=0)
                s = lax.dot_general(q4, kb, (((1,), (1,)), ((), ())),
                                    preferred_element_type=F32)
                s = (s.reshape(HEADS_PER_TILE, BLOCK, nkeys) + bias[None]
                     ).reshape(HEADS_PER_TILE * BLOCK, nkeys)
                m = jnp.max(s, axis=-1, keepdims=True)
                p = jnp.exp2(s - m)
                den = jnp.sum(p, axis=-1, keepdims=True)
                pv = jnp.dot(p.astype(BF16), vb, preferred_element_type=F32)
                o = pv[:BLOCK]
                for j in range(HEADS_PER_TILE):
                    rows = slice(j * BLOCK, (j + 1) * BLOCK)
                    col = cl * LANES + HEADS_PER_TILE * t + j
                    if j:
                        o = jnp.where(vhead == j, pv[rows], o)
                    max_ref[0, rows_q, col:col + 1] = m[rows]
                    den_ref[0, rows_q, col:col + 1] = den[rows]
                o_ref[0, rows_q, cs] = o.astype(o_ref.dtype)


def _dilated_group(q_g, k_g, v_g, g, batch, seq):
    _, r = DIL_GROUPS[g]
    sub_len = seq // r
    tl = min(sub_len, 512)
    classes = min(r, 512 // tl)
    view = (batch, sub_len, r * D_MODEL)
    kern = functools.partial(_dilated_kernel, tl=tl, has_prev=sub_len > BLOCK,
                             classes=classes)
    width = classes * D_MODEL
    stat_spec = pl.BlockSpec((1, tl, classes * LANES), lambda b, c, i: (b, i, c))
    stat_shape = jax.ShapeDtypeStruct((batch, sub_len, r * LANES), F32)
    o, mx, den = pl.pallas_call(
        kern,
        grid=(batch, r // classes, sub_len // tl),
        in_specs=[
            pl.BlockSpec((1, tl, width), lambda b, c, i: (b, i, c)),
            pl.BlockSpec((1, sub_len, width), lambda b, c, i: (b, 0, c)),
            pl.BlockSpec((1, sub_len, width), lambda b, c, i: (b, 0, c)),
        ],
        out_specs=[pl.BlockSpec((1, tl, width), lambda b, c, i: (b, i, c)),
                   stat_spec, stat_spec],
        out_shape=[jax.ShapeDtypeStruct((batch, sub_len, r * D_MODEL), BF16),
                   stat_shape, stat_shape],
        compiler_params=_params(3),
        name=f"dilated_g{g}",
    )(q_g.reshape(view), k_g.reshape(view), v_g.reshape(view))
    n = batch * seq
    return (o.reshape(n // r, r * D_MODEL), mx.reshape(n // r, r * LANES),
            den.reshape(n // r, r * LANES))


FF_CHUNK = 512


def _mlp_into(out_ref, x, g_ref, wu_ref, wd_ref):
    ms = jnp.mean(x * x, axis=-1, keepdims=True)
    h = (x * lax.rsqrt(ms + EPS) * g_ref[...]).astype(BF16)
    out_ref[...] = x
    for c in range(D_FF // FF_CHUNK):
        cs = slice(c * FF_CHUNK, (c + 1) * FF_CHUNK)
        u = jnp.dot(h, wu_ref[:, cs], preferred_element_type=F32)
        a = jnp.square(jnp.maximum(u, 0.0)).astype(BF16)
        out_ref[...] += jnp.dot(a, wd_ref[cs, :], preferred_element_type=F32)


def _attn_out_mlp_kernel(x_ref, att_ref, wo_ref, g_ref, wu_ref, wd_ref, out_ref):
    x1 = x_ref[...] + jnp.dot(att_ref[...], wo_ref[...], preferred_element_type=F32)
    _mlp_into(out_ref, x1, g_ref, wu_ref, wd_ref)


def _mix_out_mlp_kernel(x_ref, o0_ref, o1_ref, o2_ref, m0_ref, m1_ref, m2_ref,
                        d0_ref, d1_ref, d2_ref, e_ref, wo_ref, g_ref, wu_ref, wd_ref,
                        out_ref, o_scr, st_scr, *, tm):
    n_slabs = D_MODEL // LANES
    for g, (o_ref, m_ref, d_ref) in enumerate(((o1_ref, m1_ref, d1_ref),
                                               (o2_ref, m2_ref, d2_ref))):
        r = DIL_GROUPS[g + 1][1]
        for cls in range(r):
            dst = pl.ds(cls, tm // r, stride=r)
            st_scr[0, g, dst, :] = m_ref[:, cls * LANES:(cls + 1) * LANES]
            st_scr[1, g, dst, :] = d_ref[:, cls * LANES:(cls + 1) * LANES]
            for s in range(n_slabs):
                lo = cls * D_MODEL + s * LANES
                o_scr[g, s, dst, :] = o_ref[:, lo:lo + LANES].astype(F32)

    maxes = (m0_ref[...], st_scr[0, 0], st_scr[0, 1])
    dens = (d0_ref[...], st_scr[1, 0], st_scr[1, 1])
    top = jnp.maximum(jnp.maximum(maxes[0], maxes[1]), maxes[2])
    es = [jnp.exp2(m - top) for m in maxes]
    inv = 1.0 / (es[0] * dens[0] + es[1] * dens[1] + es[2] * dens[2])
    wexp = []
    for e in es:
        wg = e * inv
        hi = wg.astype(BF16)
        lo = (wg - hi.astype(F32)).astype(BF16)
        wexp.append(jnp.dot(jnp.concatenate([hi, lo], axis=1), e_ref[...],
                            preferred_element_type=F32))
    slabs = []
    for s in range(n_slabs):
        cs = slice(s * LANES, (s + 1) * LANES)
        mixed = (wexp[0][:, cs] * o0_ref[:, cs].astype(F32)
                 + wexp[1][:, cs] * o_scr[0, s] + wexp[2][:, cs] * o_scr[1, s])
        slabs.append(mixed.astype(BF16))
    x1 = x_ref[...] + jnp.dot(jnp.concatenate(slabs, axis=1), wo_ref[...],
                              preferred_element_type=F32)
    _mlp_into(out_ref, x1, g_ref, wu_ref, wd_ref)


def _resident(shape):
    return pl.BlockSpec(shape, lambda i: (0,) * len(shape), pipeline_mode=pl.Buffered(1))


def _mlp_specs():
    return [_resident((1, D_MODEL)), _resident((D_MODEL, D_FF)), _resident((D_FF, D_MODEL))]


def _attn_out_mlp(x2, att, w_o, gain, w_up, w_down, tm=512):
    n = x2.shape[0]
    row = pl.BlockSpec((tm, D_MODEL), lambda i: (i, 0))
    return pl.pallas_call(
        _attn_out_mlp_kernel,
        grid=(n // tm,),
        in_specs=[row, row, _resident((D_MODEL, D_MODEL)), *_mlp_specs()],
        out_specs=row,
        out_shape=jax.ShapeDtypeStruct((n, D_MODEL), F32),
        compiler_params=_params(1),
        name="attn_out_mlp",
    )(x2, att, w_o, gain.reshape(1, D_MODEL), w_up, w_down)


def _mix_out_mlp(x2, outs, maxes, dens, expand, w_o, gain, w_up, w_down, tm=512):
    n = x2.shape[0]
    row = pl.BlockSpec((tm, D_MODEL), lambda i: (i, 0))
    o_specs = [pl.BlockSpec((tm // r, r * D_MODEL), lambda i: (i, 0)) for _, r in DIL_GROUPS]
    st_specs = [pl.BlockSpec((tm // r, r * LANES), lambda i: (i, 0)) for _, r in DIL_GROUPS]
    return pl.pallas_call(
        functools.partial(_mix_out_mlp_kernel, tm=tm),
        grid=(n // tm,),
        in_specs=[row, *o_specs, *st_specs, *st_specs,
                  _resident((2 * LANES, D_MODEL)), _resident((D_MODEL, D_MODEL)),
                  *_mlp_specs()],
        out_specs=row,
        out_shape=jax.ShapeDtypeStruct((n, D_MODEL), F32),
        scratch_shapes=[pltpu.VMEM((N_GROUPS - 1, D_MODEL // LANES, tm, LANES), F32),
                        pltpu.VMEM((2, N_GROUPS - 1, tm, LANES), F32)],
        compiler_params=_params(1),
        name="mix_out_mlp",
    )(x2, *outs, *maxes, *dens, expand, w_o, gain.reshape(1, D_MODEL), w_up, w_down)


def _rope_tables(seq):
    inv = 1.0 / (ROPE_THETA ** (jnp.arange(0, HEAD_DIM, 2, dtype=F32) / HEAD_DIM))
    ang = jnp.arange(seq, dtype=F32)[:, None] * inv[None, :]
    reps = LANES // HALF_DIM
    return jnp.tile(jnp.cos(ang), (1, reps)), jnp.tile(jnp.sin(ang), (1, reps))


def _to_rotary_layout(a, n_rope):
    lead = a.shape[:-1]
    rot = a[..., :n_rope].reshape(*lead, n_rope // MXU_TILE, HEADS_PER_TILE, 2, HALF_DIM)
    rot = jnp.swapaxes(rot, -2, -3).reshape(*lead, n_rope)
    return jnp.concatenate([rot, a[..., n_rope:]], axis=-1)


def _head_mean_matrix():
    i = jnp.arange(MXU_TILE)
    same = i[:, None] // HALF_DIM == i[None, :] // HALF_DIM
    return (same.astype(F32) / HEAD_DIM).astype(BF16)


def _head_expand_matrix():
    r = jnp.arange(2 * LANES)[:, None] % LANES
    c = jnp.arange(D_MODEL)[None, :] // HEAD_DIM
    return (r == c).astype(BF16)


def _rotary_weights(w, head_gain, n_rope):
    return (_to_rotary_layout(w, n_rope).astype(BF16),
            _to_rotary_layout(head_gain, n_rope).reshape(1, n_rope))


def kernel(x, a_norm, a_w_qkv, a_q_gain, a_k_gain, a_lam_q1, a_lam_k1, a_lam_q2, a_lam_k2, a_sub_gain, a_w_o, kv_norm, kv_w, kv_k_gain, b_norm, b_w_q, b_q_gain, b_w_o, m_norm, m_w_up, m_w_down):
    batch, seq, _ = x.shape
    n = batch * seq
    scale = HEAD_DIM ** -0.5
    rope = _rope_tables(seq)
    gmat = _head_mean_matrix()
    expand = _head_expand_matrix()
    x2 = x.reshape(n, D_MODEL)
    gw = N_GROUPS * D_MODEL
    dil = tuple(r for _, r in DIL_GROUPS)

    k_sh = v_sh = None
    for layer in range(DEPTH):
        if layer < N_A_LAYERS:
            lam_init = 0.8 - 0.6 * math.exp(-0.3 * layer)
            hg = jnp.concatenate([jnp.tile(a_q_gain[layer] * (scale * LOG2_E), 2 * DIFF_HEADS),
                                  jnp.tile(a_k_gain[layer], 2 * DIFF_HEADS)])
            w, hg = _rotary_weights(a_w_qkv[layer], hg, 2 * D_MODEL)
            q, k, v = _project(x2, a_norm[layer], w, hg, rope, gmat, (1, 1, 1),
                               2 * D_MODEL, seq)
            att = _diff_attention(
                q, k, v,
                (a_lam_q1[layer], a_lam_k1[layer], a_lam_q2[layer], a_lam_k2[layer]),
                a_sub_gain[layer], lam_init, batch, seq)
            x2 = _attn_out_mlp(x2, att, a_w_o[layer].astype(BF16), m_norm[layer],
                               m_w_up[layer].astype(BF16), m_w_down[layer].astype(BF16))
        else:
            if layer == N_A_LAYERS:
                hg = jnp.repeat(kv_k_gain, DIL_HEADS, axis=0).reshape(gw)
                w, hg = _rotary_weights(kv_w, hg, gw)
                kv = _project(x2, kv_norm, w, hg, rope, gmat, dil + dil, gw, seq)
                k_sh, v_sh = kv[:N_GROUPS], kv[N_GROUPS:]
            bl = layer - N_A_LAYERS
            hg = jnp.repeat(b_q_gain[bl] * (scale * LOG2_E), DIL_HEADS, axis=0).reshape(gw)
            w, hg = _rotary_weights(b_w_q[bl], hg, gw)
            qs = _project(x2, b_norm[bl], w, hg, rope, gmat, dil, gw, seq)
            stats = [_dilated_group(qs[g], k_sh[g], v_sh[g], g, batch, seq)
                     for g in range(N_GROUPS)]
            outs, maxes, dens = zip(*stats)
            x2 = _mix_out_mlp(x2, outs, maxes, dens, expand, b_w_o[bl].astype(BF16),
                              m_norm[layer], m_w_up[layer].astype(BF16),
                              m_w_down[layer].astype(BF16))
    return x2.reshape(batch, seq, D_MODEL)
```

```python
import functools
import math

import jax
import jax.numpy as jnp
from jax import lax
from jax.experimental import pallas as pl
from jax.experimental.pallas import tpu as pltpu

D_MODEL = 1024
HEAD_DIM = 64
HALF_DIM = HEAD_DIM // 2
DEPTH = 4
N_A_LAYERS = DEPTH // 2
DIFF_HEADS = D_MODEL // (2 * HEAD_DIM)
DIL_GROUPS = ((128, 1), (512, 4), (2048, 16))
N_GROUPS = len(DIL_GROUPS)
DIL_HEADS = D_MODEL // HEAD_DIM
D_FF = 4 * D_MODEL
ROPE_THETA = 10000.0
BLOCK = 128
EPS = 1e-6

LANES = 128
MXU_TILE = 256
HEADS_PER_TILE = MXU_TILE // HEAD_DIM
HEAD_BLOCKED = 0
SINGLE_OP_STRIDE = 4
VMEM_LIMIT = 56 * 1024 * 1024
NEG_BIG = -1e30
LOG2_E = math.log2(math.e)

F32 = jnp.float32
BF16 = jnp.bfloat16


def _params(n_axes):
    return pltpu.CompilerParams(
        dimension_semantics=("arbitrary",) * n_axes,
        vmem_limit_bytes=VMEM_LIMIT)


def _head_of_lane(shape, axis):
    lane = lax.broadcasted_iota(jnp.int32, shape, axis)
    return (lane % LANES) // HALF_DIM


def _proj_kernel(x_ref, g_ref, rot_ref, gm_ref, *refs, n_rope, n_weights, dilations, tm):
    w_refs, refs = refs[:n_weights], refs[n_weights:]
    out_refs = refs[:len(dilations)]
    slab_scr, part_scr, h_scr = refs[len(dilations):]
    x = x_ref[...]
    ms = jnp.mean(x * x, axis=-1, keepdims=True)
    h_scr[...] = (x * lax.rsqrt(ms + EPS) * g_ref[...]).astype(BF16)
    chunk = 2 * MXU_TILE
    n_chunks = len(dilations) * D_MODEL // chunk

    def matmul(i):
        col = i * chunk
        w_ref = w_refs[0]
        if col >= w_ref.shape[1]:
            w_ref, col = w_refs[1], col - w_ref.shape[1]
        return jnp.dot(h_scr[...], w_ref[:, col:col + chunk],
                       preferred_element_type=F32)

    def epilogue(i, y):
        col = i * chunk
        out_ref, r = out_refs[col // D_MODEL], dilations[col // D_MODEL]
        slabs = [y[:, s * LANES:(s + 1) * LANES] for s in range(chunk // LANES)]
        if col < n_rope:
            sq = jnp.concatenate([slabs[0] * slabs[0] + slabs[1] * slabs[1],
                                  slabs[2] * slabs[2] + slabs[3] * slabs[3]], axis=1)
            ss = jnp.dot(sq.astype(BF16), gm_ref[...], preferred_element_type=F32)
            rs = lax.rsqrt(ss + EPS)
            sec = col // D_MODEL
            rot = []
            for t in range(2):
                rs_t = rs[:, t * LANES:(t + 1) * LANES]
                y0, y1 = slabs[2 * t], slabs[2 * t + 1]
                rot += [(y0 * rot_ref[sec, 0] - y1 * rot_ref[sec, 1]) * rs_t,
                        (y1 * rot_ref[sec, 2] + y0 * rot_ref[sec, 3]) * rs_t]
            slabs = rot
        for s, slab_val in enumerate(slabs):
            lo = col % D_MODEL + s * LANES
            if r == HEAD_BLOCKED:
                width = out_ref.shape[-1]
                out_ref[0, lo // width, :, lo % width:lo % width + LANES] = (
                    slab_val.astype(out_ref.dtype))
            elif r == 1:
                out_ref[:, lo:lo + LANES] = slab_val.astype(out_ref.dtype)
            else:
                slab = (i * (chunk // LANES) + s) % slab_scr.shape[0]
                slab_scr[slab] = slab_val
                if r <= SINGLE_OP_STRIDE:
                    for cls in range(r):
                        out_ref[:, cls * D_MODEL + lo:cls * D_MODEL + lo + LANES] = (
                            slab_scr[slab, pl.ds(cls, tm // r, stride=r), :]
                            .astype(out_ref.dtype))
                else:
                    r1 = SINGLE_OP_STRIDE
                    r2 = r // r1
                    rows1 = tm // r1
                    for c1 in range(r1):
                        part_scr[slab, c1 * rows1:(c1 + 1) * rows1, :] = (
                            slab_scr[slab, pl.ds(c1, rows1, stride=r1), :])
                    for c1 in range(r1):
                        for c2 in range(r2):
                            cls = c1 + r1 * c2
                            out_ref[:, cls * D_MODEL + lo:cls * D_MODEL + lo + LANES] = (
                                part_scr[slab, pl.ds(c1 * rows1 + c2, tm // r, stride=r2), :]
                                .astype(out_ref.dtype))

    for i in range(n_chunks):
        epilogue(i, matmul(i))


def _project(x2, gain, weights, rot_tables, gmat, dilations, seq, tm=512):
    n = x2.shape[0]
    n_sec = rot_tables.shape[0]
    assert weights[0].shape[1] == n_sec * D_MODEL
    pos_blocks = seq // tm
    kern = functools.partial(_proj_kernel, n_rope=n_sec * D_MODEL, n_weights=len(weights),
                             dilations=dilations, tm=tm)
    out_specs, out_shape = [], []
    for sec, r in enumerate(dilations):
        if r == HEAD_BLOCKED:
            width = MXU_TILE if sec < n_sec else LANES
            out_specs.append(pl.BlockSpec((1, D_MODEL // width, tm, width),
                                          lambda i: (i // pos_blocks, 0, i % pos_blocks, 0)))
            out_shape.append(jax.ShapeDtypeStruct((n // seq, D_MODEL // width, seq, width), BF16))
        else:
            out_specs.append(pl.BlockSpec((tm // r, r * D_MODEL), lambda i: (i, 0)))
            out_shape.append(jax.ShapeDtypeStruct((n // r, r * D_MODEL), BF16))
    return pl.pallas_call(
        kern,
        grid=(n // tm,),
        in_specs=[
            pl.BlockSpec((tm, D_MODEL), lambda i: (i, 0)),
            pl.BlockSpec((1, D_MODEL), lambda i: (0, 0)),
            pl.BlockSpec((n_sec, 4, tm, LANES), lambda i: (0, 0, i % pos_blocks, 0)),
            pl.BlockSpec((MXU_TILE, MXU_TILE), lambda i: (0, 0)),
            *[pl.BlockSpec(w.shape, lambda i: (0, 0)) for w in weights],
        ],
        out_specs=out_specs,
        out_shape=out_shape,
        scratch_shapes=[pltpu.VMEM((8, tm, LANES), F32), pltpu.VMEM((8, tm, LANES), F32),
                        pltpu.VMEM((tm, D_MODEL), BF16)],
        compiler_params=_params(1),
        name="proj",
    )(x2, gain.reshape(1, D_MODEL), rot_tables, gmat, *weights)


def _diff_attn_kernel(lq1_ref, lk1_ref, lq2_ref, lk2_ref, sg_ref,
                      q_ref, k_ref, v_ref, o_ref, s_scr, p_scr, linv_scr,
                      *, tq, seq, lam_init, row_chunk):
    lam = (jnp.exp(jnp.sum(lq1_ref[...] * lk1_ref[...], axis=-1, keepdims=True))
           - jnp.exp(jnp.sum(lq2_ref[...] * lk2_ref[...], axis=-1, keepdims=True))
           + lam_init)
    owner = _head_of_lane((tq, MXU_TILE), 1)
    rr = lax.broadcasted_iota(jnp.int32, (row_chunk, LANES), 0)
    cc = lax.broadcasted_iota(jnp.int32, (row_chunk, LANES), 1)
    delta = rr - cc

    for qi in reversed(range(seq // tq)):
        buf = qi % 2
        kc = (qi + 1) * tq
        q = q_ref[qi * tq:(qi + 1) * tq, :]
        zero = jnp.zeros_like(q)
        q4 = jnp.concatenate([jnp.where(owner == j, q, zero)
                              for j in range(HEADS_PER_TILE)], axis=0)
        s_scr[buf, :, :kc] = lax.dot_general(
            q4, k_ref[:kc, :], (((1,), (1,)), ((), ())),
            preferred_element_type=F32)

        for r in range(HEADS_PER_TILE * tq // row_chunk):
            rows = slice(r * row_chunk, (r + 1) * row_chunk)
            row0 = (r * row_chunk) % tq

            def tile(c):
                t = s_scr[buf, rows, c * LANES:(c + 1) * LANES]
                col0 = c * LANES - qi * tq
                if col0 + LANES - 1 > row0:
                    t = jnp.where(delta >= col0 - row0, t, NEG_BIG)
                return t

            tiles = [tile(c) for c in range(kc // LANES)]
            m = tiles[0]
            for t in tiles[1:]:
                m = jnp.maximum(m, t)
            mrow = jnp.max(m, axis=-1, keepdims=True)
            lsum = None
            for c, t in enumerate(tiles):
                p = jnp.exp2(t - mrow)
                lsum = p if lsum is None else lsum + p
                p_scr[buf, rows, c * LANES:(c + 1) * LANES] = p.astype(BF16)
            l = jnp.sum(lsum, axis=-1, keepdims=True)
            linv_scr[buf, rows, :] = jnp.broadcast_to(1.0 / l, (row_chunk, LANES))

        for hh in range(HEADS_PER_TILE // 2):
            rows = slice(hh * 2 * tq, (hh + 1) * 2 * tq)
            acc = jnp.dot(p_scr[buf, rows, :kc], v_ref[hh, :kc, :],
                          preferred_element_type=F32)
            acc = acc * linv_scr[buf, rows]
            o = acc[:tq] - lam * acc[tq:]
            ms = jnp.mean(o * o, axis=-1, keepdims=True)
            o = o * lax.rsqrt(ms + EPS) * sg_ref[...] * (1.0 - lam_init)
            o_ref[hh, qi * tq:(qi + 1) * tq, :] = o.astype(o_ref.dtype)


def _diff_attention(q, k, v, lam_vecs, sub_gain, lam_init, tq=256, row_chunk=64):
    batch, heads, seq, _ = v.shape
    pair = HEADS_PER_TILE // 2
    rows = HEADS_PER_TILE * tq
    vec = pl.BlockSpec((1, HEAD_DIM), lambda b, t: (0, 0))
    qk_spec = pl.BlockSpec((None, None, seq, MXU_TILE), lambda b, t: (b, t, 0, 0))
    head = pl.BlockSpec((None, pair, seq, LANES), lambda b, t: (b, t, 0, 0))
    kern = functools.partial(_diff_attn_kernel, tq=tq, seq=seq, lam_init=lam_init,
                             row_chunk=row_chunk)
    return pl.pallas_call(
        kern,
        grid=(batch, heads // pair),
        in_specs=[vec, vec, vec, vec,
                  pl.BlockSpec((1, 2 * HEAD_DIM), lambda b, t: (0, 0)),
                  qk_spec, qk_spec, head],
        out_specs=head,
        out_shape=jax.ShapeDtypeStruct(v.shape, BF16),
        scratch_shapes=[pltpu.VMEM((2, rows, seq), F32),
                        pltpu.VMEM((2, rows, seq), BF16),
                        pltpu.VMEM((2, rows, LANES), F32)],
        compiler_params=_params(2),
        name="diff_attn",
    )(*[u.reshape(1, HEAD_DIM) for u in lam_vecs], sub_gain.reshape(1, 2 * HEAD_DIM),
      q, k, v)


def _dilated_kernel(q_ref, k_ref, v_ref, o_ref, max_ref, den_ref, *, tl, has_prev, classes):
    li = pl.program_id(2)
    nqb = tl // BLOCK
    nkeys = 2 * BLOCK if has_prev else BLOCK
    owner = _head_of_lane((BLOCK, MXU_TILE), 1)
    vhead = lax.broadcasted_iota(jnp.int32, (BLOCK, MXU_TILE), 1) // HEAD_DIM
    row = lax.broadcasted_iota(jnp.int32, (BLOCK, nkeys), 0)
    colm = lax.broadcasted_iota(jnp.int32, (BLOCK, nkeys), 1)
    delta = row - colm

    for qb in range(nqb):
        gb = li * nqb + qb
        rows_q = slice(qb * BLOCK, (qb + 1) * BLOCK)
        if has_prev:
            first = jnp.maximum(gb - 1, 0)
            k0 = pl.multiple_of(first * BLOCK, BLOCK)
            dist = delta + (gb - first) * BLOCK
            valid = (dist >= 0) & (dist <= BLOCK)
        else:
            k0 = pl.multiple_of(gb * BLOCK, BLOCK)
            valid = delta >= 0
        bias = jnp.where(valid, 0.0, NEG_BIG)
        for cl in range(classes):
            stat_cols = slice(cl * LANES, (cl + 1) * LANES)
            max_ref[0, rows_q, stat_cols] = jnp.zeros((BLOCK, LANES), F32)
            den_ref[0, rows_q, stat_cols] = jnp.ones((BLOCK, LANES), F32)
            for t in range(D_MODEL // MXU_TILE):
                lo = cl * D_MODEL + t * MXU_TILE
                cs = slice(lo, lo + MXU_TILE)
                q = q_ref[0, rows_q, cs]
                kb = k_ref[0, pl.ds(k0, nkeys), cs]
                vb = v_ref[0, pl.ds(k0, nkeys), cs]
                zero = jnp.zeros_like(q)
                q4 = jnp.concatenate([jnp.where(owner == j, q, zero)
                                      for j in range(HEADS_PER_TILE)], axis=0)
                s = lax.dot_general(q4, kb, (((1,), (1,)), ((), ())),
                                    preferred_element_type=F32)
                s = (s.reshape(HEADS_PER_TILE, BLOCK, nkeys) + bias[None]
                     ).reshape(HEADS_PER_TILE * BLOCK, nkeys)
                m = jnp.max(s, axis=-1, keepdims=True)
                p = jnp.exp2(s - m)
                den = jnp.sum(p, axis=-1, keepdims=True)
                pv = jnp.dot(p.astype(BF16), vb, preferred_element_type=F32)
                o = pv[:BLOCK]
                for j in range(HEADS_PER_TILE):
                    rows = slice(j * BLOCK, (j + 1) * BLOCK)
                    col = cl * LANES + HEADS_PER_TILE * t + j
                    if j:
                        o = jnp.where(vhead == j, pv[rows], o)
                    max_ref[0, rows_q, col:col + 1] = m[rows]
                    den_ref[0, rows_q, col:col + 1] = den[rows]
                o_ref[0, rows_q, cs] = o.astype(o_ref.dtype)


def _dilated_group(q_g, k_g, v_g, g, batch, seq):
    _, r = DIL_GROUPS[g]
    sub_len = seq // r
    tl = min(sub_len, 512)
    classes = min(r, 512 // tl)
    view = (batch, sub_len, r * D_MODEL)
    kern = functools.partial(_dilated_kernel, tl=tl, has_prev=sub_len > BLOCK,
                             classes=classes)
    width = classes * D_MODEL
    stat_spec = pl.BlockSpec((1, tl, classes * LANES), lambda b, c, i: (b, i, c))
    stat_shape = jax.ShapeDtypeStruct((batch, sub_len, r * LANES), F32)
    o, mx, den = pl.pallas_call(
        kern,
        grid=(batch, r // classes, sub_len // tl),
        in_specs=[
            pl.BlockSpec((1, tl, width), lambda b, c, i: (b, i, c)),
            pl.BlockSpec((1, sub_len, width), lambda b, c, i: (b, 0, c)),
            pl.BlockSpec((1, sub_len, width), lambda b, c, i: (b, 0, c)),
        ],
        out_specs=[pl.BlockSpec((1, tl, width), lambda b, c, i: (b, i, c)),
                   stat_spec, stat_spec],
        out_shape=[jax.ShapeDtypeStruct((batch, sub_len, r * D_MODEL), BF16),
                   stat_shape, stat_shape],
        compiler_params=_params(3),
        name=f"dilated_g{g}",
    )(q_g.reshape(view), k_g.reshape(view), v_g.reshape(view))
    n = batch * seq
    return (o.reshape(n // r, r * D_MODEL), mx.reshape(n // r, r * LANES),
            den.reshape(n // r, r * LANES))


FF_CHUNK = 512


def _mlp_into(out_ref, x, g_ref, wu_ref, wd_ref):
    ms = jnp.mean(x * x, axis=-1, keepdims=True)
    h = (x * lax.rsqrt(ms + EPS) * g_ref[...]).astype(BF16)
    out_ref[...] = x
    for c in range(D_FF // FF_CHUNK):
        cs = slice(c * FF_CHUNK, (c + 1) * FF_CHUNK)
        u = jnp.dot(h, wu_ref[:, cs], preferred_element_type=F32)
        a = jnp.square(jnp.maximum(u, 0.0)).astype(BF16)
        out_ref[...] += jnp.dot(a, wd_ref[cs, :], preferred_element_type=F32)


def _attn_out_mlp_kernel(x_ref, att_ref, wo_ref, g_ref, wu_ref, wd_ref, out_ref):
    att = jnp.concatenate([att_ref[h] for h in range(att_ref.shape[0])], axis=1)
    x1 = x_ref[...] + jnp.dot(att, wo_ref[...], preferred_element_type=F32)
    _mlp_into(out_ref, x1, g_ref, wu_ref, wd_ref)


def _mix_out_mlp_kernel(x_ref, o0_ref, o1_ref, o2_ref, m0_ref, m1_ref, m2_ref,
                        d0_ref, d1_ref, d2_ref, e_ref, wo_ref, g_ref, wu_ref, wd_ref,
                        out_ref, o_scr, st_scr, *, tm):
    n_slabs = D_MODEL // LANES
    for g, (o_ref, m_ref, d_ref) in enumerate(((o1_ref, m1_ref, d1_ref),
                                               (o2_ref, m2_ref, d2_ref))):
        r = DIL_GROUPS[g + 1][1]
        for cls in range(r):
            dst = pl.ds(cls, tm // r, stride=r)
            st_scr[0, g, dst, :] = m_ref[:, cls * LANES:(cls + 1) * LANES]
            st_scr[1, g, dst, :] = d_ref[:, cls * LANES:(cls + 1) * LANES]
            for s in range(n_slabs):
                lo = cls * D_MODEL + s * LANES
                o_scr[g, s, dst, :] = o_ref[:, lo:lo + LANES].astype(F32)

    maxes = (m0_ref[...], st_scr[0, 0], st_scr[0, 1])
    dens = (d0_ref[...], st_scr[1, 0], st_scr[1, 1])
    top = jnp.maximum(jnp.maximum(maxes[0], maxes[1]), maxes[2])
    es = [jnp.exp2(m - top) for m in maxes]
    inv = 1.0 / (es[0] * dens[0] + es[1] * dens[1] + es[2] * dens[2])
    wexp = []
    for e in es:
        wg = e * inv
        hi = wg.astype(BF16)
        lo = (wg - hi.astype(F32)).astype(BF16)
        wexp.append(jnp.dot(jnp.concatenate([hi, lo], axis=1), e_ref[...],
                            preferred_element_type=F32))
    slabs = []
    for s in range(n_slabs):
        cs = slice(s * LANES, (s + 1) * LANES)
        mixed = (wexp[0][:, cs] * o0_ref[:, cs].astype(F32)
                 + wexp[1][:, cs] * o_scr[0, s] + wexp[2][:, cs] * o_scr[1, s])
        slabs.append(mixed.astype(BF16))
    x1 = x_ref[...] + jnp.dot(jnp.concatenate(slabs, axis=1), wo_ref[...],
                              preferred_element_type=F32)
    _mlp_into(out_ref, x1, g_ref, wu_ref, wd_ref)


def _resident(shape):
    return pl.BlockSpec(shape, lambda i: (0,) * len(shape), pipeline_mode=pl.Buffered(1))


def _mlp_specs():
    return [_resident((1, D_MODEL)), _resident((D_MODEL, D_FF)), _resident((D_FF, D_MODEL))]


def _attn_out_mlp(x2, att, w_o, gain, w_up, w_down, tm=1024):
    n = x2.shape[0]
    _, heads, seq, width = att.shape
    pos_blocks = seq // tm
    row = pl.BlockSpec((tm, D_MODEL), lambda i: (i, 0))
    att_spec = pl.BlockSpec((None, heads, tm, width),
                            lambda i: (i // pos_blocks, 0, i % pos_blocks, 0))
    return pl.pallas_call(
        _attn_out_mlp_kernel,
        grid=(n // tm,),
        in_specs=[row, att_spec, _resident((D_MODEL, D_MODEL)), *_mlp_specs()],
        out_specs=row,
        out_shape=jax.ShapeDtypeStruct((n, D_MODEL), F32),
        compiler_params=_params(1),
        name="attn_out_mlp",
    )(x2, att, w_o, gain.reshape(1, D_MODEL), w_up, w_down)


def _mix_out_mlp(x2, outs, maxes, dens, expand, w_o, gain, w_up, w_down, tm=512):
    n = x2.shape[0]
    row = pl.BlockSpec((tm, D_MODEL), lambda i: (i, 0))
    o_specs = [pl.BlockSpec((tm // r, r * D_MODEL), lambda i: (i, 0)) for _, r in DIL_GROUPS]
    st_specs = [pl.BlockSpec((tm // r, r * LANES), lambda i: (i, 0)) for _, r in DIL_GROUPS]
    return pl.pallas_call(
        functools.partial(_mix_out_mlp_kernel, tm=tm),
        grid=(n // tm,),
        in_specs=[row, *o_specs, *st_specs, *st_specs,
                  _resident((2 * LANES, D_MODEL)), _resident((D_MODEL, D_MODEL)),
                  *_mlp_specs()],
        out_specs=row,
        out_shape=jax.ShapeDtypeStruct((n, D_MODEL), F32),
        scratch_shapes=[pltpu.VMEM((N_GROUPS - 1, D_MODEL // LANES, tm, LANES), F32),
                        pltpu.VMEM((2, N_GROUPS - 1, tm, LANES), F32)],
        compiler_params=_params(1),
        name="mix_out_mlp",
    )(x2, *outs, *maxes, *dens, expand, w_o, gain.reshape(1, D_MODEL), w_up, w_down)


def _rope_tables(seq):
    inv = 1.0 / (ROPE_THETA ** (jnp.arange(0, HEAD_DIM, 2, dtype=F32) / HEAD_DIM))
    ang = jnp.arange(seq, dtype=F32)[:, None] * inv[None, :]
    planes = jnp.stack([jnp.cos(ang), jnp.sin(ang), jnp.cos(ang), jnp.sin(ang)])
    return jnp.tile(planes, (1, 1, LANES // HALF_DIM))


def _rotary_gain_tables(rope, gains):
    g_lo, g_hi = gains[:, :HALF_DIM], gains[:, HALF_DIM:]
    g = jnp.tile(jnp.stack([g_lo, g_hi, g_hi, g_lo], axis=1), (1, 1, LANES // HALF_DIM))
    return g[:, :, None, :] * rope[None]


def _to_rotary_layout(w):
    rows, cols = w.shape
    w = w.reshape(rows, cols // MXU_TILE, HEADS_PER_TILE, 2, HALF_DIM)
    return jnp.swapaxes(w, -2, -3).reshape(rows, cols)


def _head_mean_matrix():
    i = jnp.arange(MXU_TILE)
    same = i[:, None] // HALF_DIM == i[None, :] // HALF_DIM
    return (same.astype(F32) / HEAD_DIM).astype(BF16)


def _head_expand_matrix():
    r = jnp.arange(2 * LANES)[:, None] % LANES
    c = jnp.arange(D_MODEL)[None, :] // HEAD_DIM
    return (r == c).astype(BF16)


def kernel(x, a_norm, a_w_qkv, a_q_gain, a_k_gain, a_lam_q1, a_lam_k1, a_lam_q2, a_lam_k2, a_sub_gain, a_w_o, kv_norm, kv_w, kv_k_gain, b_norm, b_w_q, b_q_gain, b_w_o, m_norm, m_w_up, m_w_down):
    batch, seq, _ = x.shape
    n = batch * seq
    scale = HEAD_DIM ** -0.5
    rope = _rope_tables(seq)
    gmat = _head_mean_matrix()
    expand = _head_expand_matrix()
    x2 = x.reshape(n, D_MODEL)
    gw = N_GROUPS * D_MODEL
    dil = tuple(r for _, r in DIL_GROUPS)

    k_sh = v_sh = None
    for layer in range(DEPTH):
        if layer < N_A_LAYERS:
            lam_init = 0.8 - 0.6 * math.exp(-0.3 * layer)
            tables = _rotary_gain_tables(
                rope, jnp.stack([a_q_gain[layer] * (scale * LOG2_E), a_k_gain[layer]]))
            w_qkv = a_w_qkv[layer]
            weights = (_to_rotary_layout(w_qkv[:, :2 * D_MODEL]).astype(BF16),
                       w_qkv[:, 2 * D_MODEL:].astype(BF16))
            q, k, v = _project(x2, a_norm[layer], weights, tables, gmat,
                               (HEAD_BLOCKED,) * 3, seq)
            att = _diff_attention(
                q, k, v,
                (a_lam_q1[layer], a_lam_k1[layer], a_lam_q2[layer], a_lam_k2[layer]),
                a_sub_gain[layer], lam_init)
            x2 = _attn_out_mlp(x2, att, a_w_o[layer].astype(BF16), m_norm[layer],
                               m_w_up[layer].astype(BF16), m_w_down[layer].astype(BF16))
        else:
            if layer == N_A_LAYERS:
                weights = (_to_rotary_layout(kv_w[:, :gw]).astype(BF16),
                           kv_w[:, gw:].astype(BF16))
                kv = _project(x2, kv_norm, weights, _rotary_gain_tables(rope, kv_k_gain),
                              gmat, dil + dil, seq)
                k_sh, v_sh = kv[:N_GROUPS], kv[N_GROUPS:]
            bl = layer - N_A_LAYERS
            weights = (_to_rotary_layout(b_w_q[bl]).astype(BF16),)
            tables = _rotary_gain_tables(rope, b_q_gain[bl] * (scale * LOG2_E))
            qs = _project(x2, b_norm[bl], weights, tables, gmat, dil, seq)
            stats = [_dilated_group(qs[g], k_sh[g], v_sh[g], g, batch, seq)
                     for g in range(N_GROUPS)]
            outs, maxes, dens = zip(*stats)
            x2 = _mix_out_mlp(x2, outs, maxes, dens, expand, b_w_o[bl].astype(BF16),
                              m_norm[layer], m_w_up[layer].astype(BF16),
                              m_w_down[layer].astype(BF16))
    return x2.reshape(batch, seq, D_MODEL)
```

```python
import functools
import math

import jax
import jax.numpy as jnp
from jax import lax
from jax.experimental import pallas as pl
from jax.experimental.pallas import tpu as pltpu

D_MODEL = 1024
HEAD_DIM = 64
HALF_DIM = HEAD_DIM // 2
DEPTH = 4
N_A_LAYERS = DEPTH // 2
DIFF_HEADS = D_MODEL // (2 * HEAD_DIM)
DIL_GROUPS = ((128, 1), (512, 4), (2048, 16))
N_GROUPS = len(DIL_GROUPS)
DIL_HEADS = D_MODEL // HEAD_DIM
D_FF = 4 * D_MODEL
ROPE_THETA = 10000.0
BLOCK = 128
EPS = 1e-6

LANES = 128
MXU_TILE = 256
HEADS_PER_TILE = MXU_TILE // HEAD_DIM
HEAD_BLOCKED = 0
SINGLE_OP_STRIDE = 4
VMEM_LIMIT = 56 * 1024 * 1024
NEG_BIG = -1e30
LOG2_E = math.log2(math.e)

F32 = jnp.float32
BF16 = jnp.bfloat16


def _params(n_axes):
    return pltpu.CompilerParams(
        dimension_semantics=("arbitrary",) * n_axes,
        vmem_limit_bytes=VMEM_LIMIT)


def _head_of_lane(shape, axis):
    lane = lax.broadcasted_iota(jnp.int32, shape, axis)
    return (lane % LANES) // HALF_DIM


def _proj_kernel(x_ref, g_ref, rot_ref, gm_ref, *refs, n_rope, n_weights, dilations, tm):
    w_refs, refs = refs[:n_weights], refs[n_weights:]
    out_refs = refs[:len(dilations)]
    slab_scr, part_scr, h_scr = refs[len(dilations):]
    x = x_ref[...]
    ms = jnp.mean(x * x, axis=-1, keepdims=True)
    h_scr[...] = (x * lax.rsqrt(ms + EPS) * g_ref[...]).astype(BF16)
    chunk = 2 * MXU_TILE
    n_chunks = len(dilations) * D_MODEL // chunk

    def matmul(i):
        col = i * chunk
        w_ref = w_refs[0]
        if col >= w_ref.shape[1]:
            w_ref, col = w_refs[1], col - w_ref.shape[1]
        return jnp.dot(h_scr[...], w_ref[:, col:col + chunk],
                       preferred_element_type=F32)

    def epilogue(i, y):
        col = i * chunk
        out_ref, r = out_refs[col // D_MODEL], dilations[col // D_MODEL]
        slabs = [y[:, s * LANES:(s + 1) * LANES] for s in range(chunk // LANES)]
        if col < n_rope:
            sq = jnp.concatenate([slabs[0] * slabs[0] + slabs[1] * slabs[1],
                                  slabs[2] * slabs[2] + slabs[3] * slabs[3]], axis=1)
            ss = jnp.dot(sq.astype(BF16), gm_ref[...], preferred_element_type=F32)
            rs = lax.rsqrt(ss + EPS)
            sec = col // D_MODEL
            rot = []
            for t in range(2):
                rs_t = rs[:, t * LANES:(t + 1) * LANES]
                y0, y1 = slabs[2 * t], slabs[2 * t + 1]
                rot += [(y0 * rot_ref[sec, 0] - y1 * rot_ref[sec, 1]) * rs_t,
                        (y1 * rot_ref[sec, 2] + y0 * rot_ref[sec, 3]) * rs_t]
            slabs = rot
        for s, slab_val in enumerate(slabs):
            lo = col % D_MODEL + s * LANES
            if r == HEAD_BLOCKED:
                width = out_ref.shape[-1]
                out_ref[0, lo // width, :, lo % width:lo % width + LANES] = (
                    slab_val.astype(out_ref.dtype))
            elif r == 1:
                out_ref[:, lo:lo + LANES] = slab_val.astype(out_ref.dtype)
            else:
                slab = (i * (chunk // LANES) + s) % slab_scr.shape[0]
                slab_scr[slab] = slab_val
                if r <= SINGLE_OP_STRIDE:
                    for cls in range(r):
                        out_ref[:, cls * D_MODEL + lo:cls * D_MODEL + lo + LANES] = (
                            slab_scr[slab, pl.ds(cls, tm // r, stride=r), :]
                            .astype(out_ref.dtype))
                else:
                    r1 = SINGLE_OP_STRIDE
                    r2 = r // r1
                    rows1 = tm // r1
                    for c1 in range(r1):
                        part_scr[slab, c1 * rows1:(c1 + 1) * rows1, :] = (
                            slab_scr[slab, pl.ds(c1, rows1, stride=r1), :])
                    for c1 in range(r1):
                        for c2 in range(r2):
                            cls = c1 + r1 * c2
                            out_ref[:, cls * D_MODEL + lo:cls * D_MODEL + lo + LANES] = (
                                part_scr[slab, pl.ds(c1 * rows1 + c2, tm // r, stride=r2), :]
                                .astype(out_ref.dtype))

    for i in range(n_chunks):
        epilogue(i, matmul(i))


def _layer_block(stack, layer):
    return pl.BlockSpec((None,) + stack.shape[1:], lambda i: (layer, 0, 0),
                        pipeline_mode=pl.Buffered(1))


def _project(x2, gain, weights, layer, rot_tables, gmat, dilations, seq, tm=512):
    n = x2.shape[0]
    n_sec = rot_tables.shape[0]
    assert weights[0].shape[2] == n_sec * D_MODEL
    pos_blocks = seq // tm
    kern = functools.partial(_proj_kernel, n_rope=n_sec * D_MODEL, n_weights=len(weights),
                             dilations=dilations, tm=tm)
    out_specs, out_shape = [], []
    for sec, r in enumerate(dilations):
        if r == HEAD_BLOCKED:
            width = MXU_TILE if sec < n_sec else LANES
            out_specs.append(pl.BlockSpec((1, D_MODEL // width, tm, width),
                                          lambda i: (i // pos_blocks, 0, i % pos_blocks, 0)))
            out_shape.append(jax.ShapeDtypeStruct((n // seq, D_MODEL // width, seq, width), BF16))
        else:
            out_specs.append(pl.BlockSpec((tm // r, r * D_MODEL), lambda i: (i, 0)))
            out_shape.append(jax.ShapeDtypeStruct((n // r, r * D_MODEL), BF16))
    return pl.pallas_call(
        kern,
        grid=(n // tm,),
        in_specs=[
            pl.BlockSpec((tm, D_MODEL), lambda i: (i, 0)),
            pl.BlockSpec((1, D_MODEL), lambda i: (0, 0)),
            pl.BlockSpec((n_sec, 4, tm, LANES), lambda i: (0, 0, i % pos_blocks, 0)),
            pl.BlockSpec((MXU_TILE, MXU_TILE), lambda i: (0, 0)),
            *[_layer_block(w, layer) for w in weights],
        ],
        out_specs=out_specs,
        out_shape=out_shape,
        scratch_shapes=[pltpu.VMEM((8, tm, LANES), F32), pltpu.VMEM((8, tm, LANES), F32),
                        pltpu.VMEM((tm, D_MODEL), BF16)],
        compiler_params=_params(1),
        name="proj",
    )(x2, gain.reshape(1, D_MODEL), rot_tables, gmat, *weights)


def _diff_attn_kernel(lq1_ref, lk1_ref, lq2_ref, lk2_ref, sg_ref,
                      q_ref, k_ref, v_ref, o_ref, s_scr, p_scr, linv_scr,
                      *, tq, seq, lam_init, row_chunk):
    lam = (jnp.exp(jnp.sum(lq1_ref[...] * lk1_ref[...], axis=-1, keepdims=True))
           - jnp.exp(jnp.sum(lq2_ref[...] * lk2_ref[...], axis=-1, keepdims=True))
           + lam_init)
    owner = _head_of_lane((tq, MXU_TILE), 1)
    rr = lax.broadcasted_iota(jnp.int32, (row_chunk, LANES), 0)
    cc = lax.broadcasted_iota(jnp.int32, (row_chunk, LANES), 1)
    delta = rr - cc

    for qi in reversed(range(seq // tq)):
        buf = qi % 2
        kc = (qi + 1) * tq
        q = q_ref[qi * tq:(qi + 1) * tq, :]
        zero = jnp.zeros_like(q)
        q4 = jnp.concatenate([jnp.where(owner == j, q, zero)
                              for j in range(HEADS_PER_TILE)], axis=0)
        s_scr[buf, :, :kc] = lax.dot_general(
            q4, k_ref[:kc, :], (((1,), (1,)), ((), ())),
            preferred_element_type=F32)

        for r in range(HEADS_PER_TILE * tq // row_chunk):
            rows = slice(r * row_chunk, (r + 1) * row_chunk)
            row0 = (r * row_chunk) % tq

            def tile(c):
                t = s_scr[buf, rows, c * LANES:(c + 1) * LANES]
                col0 = c * LANES - qi * tq
                if col0 + LANES - 1 > row0:
                    t = jnp.where(delta >= col0 - row0, t, NEG_BIG)
                return t

            tiles = [tile(c) for c in range(kc // LANES)]
            m = tiles[0]
            for t in tiles[1:]:
                m = jnp.maximum(m, t)
            mrow = jnp.max(m, axis=-1, keepdims=True)
            lsum = None
            for c, t in enumerate(tiles):
                p = jnp.exp2(t - mrow)
                lsum = p if lsum is None else lsum + p
                p_scr[buf, rows, c * LANES:(c + 1) * LANES] = p.astype(BF16)
            l = jnp.sum(lsum, axis=-1, keepdims=True)
            linv_scr[buf, rows, :] = jnp.broadcast_to(1.0 / l, (row_chunk, LANES))

        for hh in range(HEADS_PER_TILE // 2):
            rows = slice(hh * 2 * tq, (hh + 1) * 2 * tq)
            acc = jnp.dot(p_scr[buf, rows, :kc], v_ref[hh, :kc, :],
                          preferred_element_type=F32)
            acc = acc * linv_scr[buf, rows]
            o = acc[:tq] - lam * acc[tq:]
            ms = jnp.mean(o * o, axis=-1, keepdims=True)
            o = o * lax.rsqrt(ms + EPS) * sg_ref[...] * (1.0 - lam_init)
            o_ref[hh, qi * tq:(qi + 1) * tq, :] = o.astype(o_ref.dtype)


def _diff_attention(q, k, v, lam_vecs, sub_gain, lam_init, tq=256, row_chunk=64):
    batch, heads, seq, _ = v.shape
    pair = HEADS_PER_TILE // 2
    rows = HEADS_PER_TILE * tq
    vec = pl.BlockSpec((1, HEAD_DIM), lambda b, t: (0, 0))
    qk_spec = pl.BlockSpec((None, None, seq, MXU_TILE), lambda b, t: (b, t, 0, 0))
    head = pl.BlockSpec((None, pair, seq, LANES), lambda b, t: (b, t, 0, 0))
    kern = functools.partial(_diff_attn_kernel, tq=tq, seq=seq, lam_init=lam_init,
                             row_chunk=row_chunk)
    return pl.pallas_call(
        kern,
        grid=(batch, heads // pair),
        in_specs=[vec, vec, vec, vec,
                  pl.BlockSpec((1, 2 * HEAD_DIM), lambda b, t: (0, 0)),
                  qk_spec, qk_spec, head],
        out_specs=head,
        out_shape=jax.ShapeDtypeStruct(v.shape, BF16),
        scratch_shapes=[pltpu.VMEM((2, rows, seq), F32),
                        pltpu.VMEM((2, rows, seq), BF16),
                        pltpu.VMEM((2, rows, LANES), F32)],
        compiler_params=_params(2),
        name="diff_attn",
    )(*[u.reshape(1, HEAD_DIM) for u in lam_vecs], sub_gain.reshape(1, 2 * HEAD_DIM),
      q, k, v)


def _dilated_kernel(q_ref, k_ref, v_ref, o_ref, max_ref, den_ref, *, tl, has_prev, classes):
    li = pl.program_id(2)
    nqb = tl // BLOCK
    nkeys = 2 * BLOCK if has_prev else BLOCK
    owner = _head_of_lane((BLOCK, MXU_TILE), 1)
    vhead = lax.broadcasted_iota(jnp.int32, (BLOCK, MXU_TILE), 1) // HEAD_DIM
    row = lax.broadcasted_iota(jnp.int32, (BLOCK, nkeys), 0)
    colm = lax.broadcasted_iota(jnp.int32, (BLOCK, nkeys), 1)
    delta = row - colm

    for qb in range(nqb):
        gb = li * nqb + qb
        rows_q = slice(qb * BLOCK, (qb + 1) * BLOCK)
        if has_prev:
            first = jnp.maximum(gb - 1, 0)
            k0 = pl.multiple_of(first * BLOCK, BLOCK)
            dist = delta + (gb - first) * BLOCK
            valid = (dist >= 0) & (dist <= BLOCK)
        else:
            k0 = pl.multiple_of(gb * BLOCK, BLOCK)
            valid = delta >= 0
        bias = jnp.where(valid, 0.0, NEG_BIG)
        for cl in range(classes):
            stat_cols = slice(cl * LANES, (cl + 1) * LANES)
            max_ref[0, rows_q, stat_cols] = jnp.zeros((BLOCK, LANES), F32)
            den_ref[0, rows_q, stat_cols] = jnp.ones((BLOCK, LANES), F32)
            for t in range(D_MODEL // MXU_TILE):
                lo = cl * D_MODEL + t * MXU_TILE
                cs = slice(lo, lo + MXU_TILE)
                q = q_ref[0, rows_q, cs]
                kb = k_ref[0, pl.ds(k0, nkeys), cs]
                vb = v_ref[0, pl.ds(k0, nkeys), cs]
                zero = jnp.zeros_like(q)
                q4 = jnp.concatenate([jnp.where(owner == j, q, zero)
                                      for j in range(HEADS_PER_TILE)], axis=0)
                s = lax.dot_general(q4, kb, (((1,), (1,)), ((), ())),
                                    preferred_element_type=F32)
                s = (s.reshape(HEADS_PER_TILE, BLOCK, nkeys) + bias[None]
                     ).reshape(HEADS_PER_TILE * BLOCK, nkeys)
                m = jnp.max(s, axis=-1, keepdims=True)
                p = jnp.exp2(s - m)
                den = jnp.sum(p, axis=-1, keepdims=True)
                pv = jnp.dot(p.astype(BF16), vb, preferred_element_type=F32)
                o = pv[:BLOCK]
                for j in range(HEADS_PER_TILE):
                    rows = slice(j * BLOCK, (j + 1) * BLOCK)
                    col = cl * LANES + HEADS_PER_TILE * t + j
                    if j:
                        o = jnp.where(vhead == j, pv[rows], o)
                    max_ref[0, rows_q, col:col + 1] = m[rows]
                    den_ref[0, rows_q, col:col + 1] = den[rows]
                o_ref[0, rows_q, cs] = o.astype(o_ref.dtype)


def _dilated_group(q_g, k_g, v_g, g, batch, seq):
    _, r = DIL_GROUPS[g]
    sub_len = seq // r
    tl = min(sub_len, 512)
    classes = min(r, 512 // tl)
    view = (batch, sub_len, r * D_MODEL)
    kern = functools.partial(_dilated_kernel, tl=tl, has_prev=sub_len > BLOCK,
                             classes=classes)
    width = classes * D_MODEL
    stat_spec = pl.BlockSpec((1, tl, classes * LANES), lambda b, c, i: (b, i, c))
    stat_shape = jax.ShapeDtypeStruct((batch, sub_len, r * LANES), F32)
    o, mx, den = pl.pallas_call(
        kern,
        grid=(batch, r // classes, sub_len // tl),
        in_specs=[
            pl.BlockSpec((1, tl, width), lambda b, c, i: (b, i, c)),
            pl.BlockSpec((1, sub_len, width), lambda b, c, i: (b, 0, c)),
            pl.BlockSpec((1, sub_len, width), lambda b, c, i: (b, 0, c)),
        ],
        out_specs=[pl.BlockSpec((1, tl, width), lambda b, c, i: (b, i, c)),
                   stat_spec, stat_spec],
        out_shape=[jax.ShapeDtypeStruct((batch, sub_len, r * D_MODEL), BF16),
                   stat_shape, stat_shape],
        compiler_params=_params(3),
        name=f"dilated_g{g}",
    )(q_g.reshape(view), k_g.reshape(view), v_g.reshape(view))
    n = batch * seq
    return (o.reshape(n // r, r * D_MODEL), mx.reshape(n // r, r * LANES),
            den.reshape(n // r, r * LANES))


FF_CHUNK = 512


def _mlp_into(out_ref, x, g_ref, wu_ref, wd_ref):
    ms = jnp.mean(x * x, axis=-1, keepdims=True)
    h = (x * lax.rsqrt(ms + EPS) * g_ref[...]).astype(BF16)
    out_ref[...] = x
    for c in range(D_FF // FF_CHUNK):
        cs = slice(c * FF_CHUNK, (c + 1) * FF_CHUNK)
        u = jnp.dot(h, wu_ref[:, cs], preferred_element_type=F32)
        a = jnp.square(jnp.maximum(u, 0.0)).astype(BF16)
        out_ref[...] += jnp.dot(a, wd_ref[cs, :], preferred_element_type=F32)


def _attn_out_mlp_kernel(x_ref, att_ref, wo_ref, g_ref, wu_ref, wd_ref, out_ref):
    att = jnp.concatenate([att_ref[h] for h in range(att_ref.shape[0])], axis=1)
    x1 = x_ref[...] + jnp.dot(att, wo_ref[...], preferred_element_type=F32)
    _mlp_into(out_ref, x1, g_ref, wu_ref, wd_ref)


def _mix_out_mlp_kernel(x_ref, o0_ref, o1_ref, o2_ref, m0_ref, m1_ref, m2_ref,
                        d0_ref, d1_ref, d2_ref, e_ref, wo_ref, g_ref, wu_ref, wd_ref,
                        out_ref, o_scr, st_scr, *, tm):
    n_slabs = D_MODEL // LANES
    for g, (o_ref, m_ref, d_ref) in enumerate(((o1_ref, m1_ref, d1_ref),
                                               (o2_ref, m2_ref, d2_ref))):
        r = DIL_GROUPS[g + 1][1]
        for cls in range(r):
            dst = pl.ds(cls, tm // r, stride=r)
            st_scr[0, g, dst, :] = m_ref[:, cls * LANES:(cls + 1) * LANES]
            st_scr[1, g, dst, :] = d_ref[:, cls * LANES:(cls + 1) * LANES]
            for s in range(n_slabs):
                lo = cls * D_MODEL + s * LANES
                o_scr[g, s, dst, :] = o_ref[:, lo:lo + LANES].astype(F32)

    maxes = (m0_ref[...], st_scr[0, 0], st_scr[0, 1])
    dens = (d0_ref[...], st_scr[1, 0], st_scr[1, 1])
    top = jnp.maximum(jnp.maximum(maxes[0], maxes[1]), maxes[2])
    es = [jnp.exp2(m - top) for m in maxes]
    inv = 1.0 / (es[0] * dens[0] + es[1] * dens[1] + es[2] * dens[2])
    wexp = []
    for e in es:
        wg = e * inv
        hi = wg.astype(BF16)
        lo = (wg - hi.astype(F32)).astype(BF16)
        wexp.append(jnp.dot(jnp.concatenate([hi, lo], axis=1), e_ref[...],
                            preferred_element_type=F32))
    slabs = []
    for s in range(n_slabs):
        cs = slice(s * LANES, (s + 1) * LANES)
        mixed = (wexp[0][:, cs] * o0_ref[:, cs].astype(F32)
                 + wexp[1][:, cs] * o_scr[0, s] + wexp[2][:, cs] * o_scr[1, s])
        slabs.append(mixed.astype(BF16))
    x1 = x_ref[...] + jnp.dot(jnp.concatenate(slabs, axis=1), wo_ref[...],
                              preferred_element_type=F32)
    _mlp_into(out_ref, x1, g_ref, wu_ref, wd_ref)


def _resident(shape):
    return pl.BlockSpec(shape, lambda i: (0,) * len(shape), pipeline_mode=pl.Buffered(1))


def _attn_out_mlp(x2, att, w_o, attn_layer, gain, w_up, w_down, layer, tm=1024):
    n = x2.shape[0]
    _, heads, seq, width = att.shape
    pos_blocks = seq // tm
    row = pl.BlockSpec((tm, D_MODEL), lambda i: (i, 0))
    att_spec = pl.BlockSpec((None, heads, tm, width),
                            lambda i: (i // pos_blocks, 0, i % pos_blocks, 0))
    return pl.pallas_call(
        _attn_out_mlp_kernel,
        grid=(n // tm,),
        in_specs=[row, att_spec, _layer_block(w_o, attn_layer), _resident((1, D_MODEL)),
                  _layer_block(w_up, layer), _layer_block(w_down, layer)],
        out_specs=row,
        out_shape=jax.ShapeDtypeStruct((n, D_MODEL), F32),
        compiler_params=_params(1),
        name="attn_out_mlp",
    )(x2, att, w_o, gain.reshape(1, D_MODEL), w_up, w_down)


def _mix_out_mlp(x2, outs, maxes, dens, expand, w_o, attn_layer, gain, w_up, w_down, layer,
                 tm=512):
    n = x2.shape[0]
    row = pl.BlockSpec((tm, D_MODEL), lambda i: (i, 0))
    o_specs = [pl.BlockSpec((tm // r, r * D_MODEL), lambda i: (i, 0)) for _, r in DIL_GROUPS]
    st_specs = [pl.BlockSpec((tm // r, r * LANES), lambda i: (i, 0)) for _, r in DIL_GROUPS]
    return pl.pallas_call(
        functools.partial(_mix_out_mlp_kernel, tm=tm),
        grid=(n // tm,),
        in_specs=[row, *o_specs, *st_specs, *st_specs,
                  _resident((2 * LANES, D_MODEL)), _layer_block(w_o, attn_layer),
                  _resident((1, D_MODEL)), _layer_block(w_up, layer),
                  _layer_block(w_down, layer)],
        out_specs=row,
        out_shape=jax.ShapeDtypeStruct((n, D_MODEL), F32),
        scratch_shapes=[pltpu.VMEM((N_GROUPS - 1, D_MODEL // LANES, tm, LANES), F32),
                        pltpu.VMEM((2, N_GROUPS - 1, tm, LANES), F32)],
        compiler_params=_params(1),
        name="mix_out_mlp",
    )(x2, *outs, *maxes, *dens, expand, w_o, gain.reshape(1, D_MODEL), w_up, w_down)


def _rope_tables(seq):
    inv = 1.0 / (ROPE_THETA ** (jnp.arange(0, HEAD_DIM, 2, dtype=F32) / HEAD_DIM))
    ang = jnp.arange(seq, dtype=F32)[:, None] * inv[None, :]
    planes = jnp.stack([jnp.cos(ang), jnp.sin(ang), jnp.cos(ang), jnp.sin(ang)])
    return jnp.tile(planes, (1, 1, LANES // HALF_DIM))


def _rotary_gain_tables(rope, gains):
    g_lo, g_hi = gains[:, :HALF_DIM], gains[:, HALF_DIM:]
    g = jnp.tile(jnp.stack([g_lo, g_hi, g_hi, g_lo], axis=1), (1, 1, LANES // HALF_DIM))
    return g[:, :, None, :] * rope[None]


def _to_rotary_layout(w):
    layers, rows, cols = w.shape
    new = jnp.arange(MXU_TILE)
    old = ((new % LANES) // HALF_DIM) * HEAD_DIM + (new // LANES) * HALF_DIM + new % HALF_DIM
    perm = (jnp.arange(MXU_TILE)[:, None] == old[None, :]).astype(BF16)
    tiles = w.astype(BF16).reshape(layers, rows, cols // MXU_TILE, MXU_TILE)
    out = jnp.einsum('lrtc,cd->lrtd', tiles, perm, preferred_element_type=BF16)
    return out.reshape(layers, rows, cols)


def _head_mean_matrix():
    i = jnp.arange(MXU_TILE)
    same = i[:, None] // HALF_DIM == i[None, :] // HALF_DIM
    return (same.astype(F32) / HEAD_DIM).astype(BF16)


def _head_expand_matrix():
    r = jnp.arange(2 * LANES)[:, None] % LANES
    c = jnp.arange(D_MODEL)[None, :] // HEAD_DIM
    return (r == c).astype(BF16)


def kernel(x, a_norm, a_w_qkv, a_q_gain, a_k_gain, a_lam_q1, a_lam_k1, a_lam_q2, a_lam_k2, a_sub_gain, a_w_o, kv_norm, kv_w, kv_k_gain, b_norm, b_w_q, b_q_gain, b_w_o, m_norm, m_w_up, m_w_down):
    batch, seq, _ = x.shape
    n = batch * seq
    scale = HEAD_DIM ** -0.5
    rope = _rope_tables(seq)
    gmat = _head_mean_matrix()
    expand = _head_expand_matrix()
    x2 = x.reshape(n, D_MODEL)
    gw = N_GROUPS * D_MODEL
    dil = tuple(r for _, r in DIL_GROUPS)

    a_w = (_to_rotary_layout(a_w_qkv[:, :, :2 * D_MODEL]), a_w_qkv[:, :, 2 * D_MODEL:].astype(BF16))
    kv_weights = (_to_rotary_layout(kv_w[None, :, :gw]), kv_w[None, :, gw:].astype(BF16))
    b_w = (_to_rotary_layout(b_w_q),)
    a_wo, b_wo = a_w_o.astype(BF16), b_w_o.astype(BF16)
    w_up, w_down = m_w_up.astype(BF16), m_w_down.astype(BF16)

    k_sh = v_sh = None
    for layer in range(DEPTH):
        if layer < N_A_LAYERS:
            lam_init = 0.8 - 0.6 * math.exp(-0.3 * layer)
            tables = _rotary_gain_tables(
                rope, jnp.stack([a_q_gain[layer] * (scale * LOG2_E), a_k_gain[layer]]))
            q, k, v = _project(x2, a_norm[layer], a_w, layer, tables, gmat,
                               (HEAD_BLOCKED,) * 3, seq)
            att = _diff_attention(
                q, k, v,
                (a_lam_q1[layer], a_lam_k1[layer], a_lam_q2[layer], a_lam_k2[layer]),
                a_sub_gain[layer], lam_init)
            x2 = _attn_out_mlp(x2, att, a_wo, layer, m_norm[layer], w_up, w_down, layer)
        else:
            if layer == N_A_LAYERS:
                kv = _project(x2, kv_norm, kv_weights, 0, _rotary_gain_tables(rope, kv_k_gain),
                              gmat, dil + dil, seq)
                k_sh, v_sh = kv[:N_GROUPS], kv[N_GROUPS:]
            bl = layer - N_A_LAYERS
            tables = _rotary_gain_tables(rope, b_q_gain[bl] * (scale * LOG2_E))
            qs = _project(x2, b_norm[bl], b_w, bl, tables, gmat, dil, seq)
            stats = [_dilated_group(qs[g], k_sh[g], v_sh[g], g, batch, seq)
                     for g in range(N_GROUPS)]
            outs, maxes, dens = zip(*stats)
            x2 = _mix_out_mlp(x2, outs, maxes, dens, expand, b_wo, bl, m_norm[layer],
                              w_up, w_down, layer)
    return x2.reshape(batch, seq, D_MODEL)
```

```python
import functools
import math

import jax
import jax.numpy as jnp
from jax import lax
from jax.experimental import pallas as pl
from jax.experimental.pallas import tpu as pltpu

D_MODEL = 1024
HEAD_DIM = 64
HALF_DIM = HEAD_DIM // 2
DEPTH = 4
N_A_LAYERS = DEPTH // 2
DIFF_HEADS = D_MODEL // (2 * HEAD_DIM)
DIL_GROUPS = ((128, 1), (512, 4), (2048, 16))
N_GROUPS = len(DIL_GROUPS)
DIL_HEADS = D_MODEL // HEAD_DIM
D_FF = 4 * D_MODEL
ROPE_THETA = 10000.0
BLOCK = 128
EPS = 1e-6

LANES = 128
MXU_TILE = 256
HEADS_PER_TILE = MXU_TILE // HEAD_DIM
HEAD_BLOCKED = 0
SINGLE_OP_STRIDE = 4
VMEM_LIMIT = 56 * 1024 * 1024
NEG_BIG = -1e30
LOG2_E = math.log2(math.e)

F32 = jnp.float32
BF16 = jnp.bfloat16


def _params(n_axes):
    return pltpu.CompilerParams(
        dimension_semantics=("arbitrary",) * n_axes,
        vmem_limit_bytes=VMEM_LIMIT)


def _head_of_lane(shape, axis):
    lane = lax.broadcasted_iota(jnp.int32, shape, axis)
    return (lane % LANES) // HALF_DIM


def _proj_kernel(x_ref, g_ref, rot_ref, gm_ref, *refs, n_rope, n_weights, dilations, tm):
    w_refs, refs = refs[:n_weights], refs[n_weights:]
    out_refs = refs[:len(dilations)]
    slab_scr, part_scr, h_scr = refs[len(dilations):]
    x = x_ref[...]
    ms = jnp.mean(x * x, axis=-1, keepdims=True)
    h_scr[...] = (x * lax.rsqrt(ms + EPS) * g_ref[...]).astype(BF16)
    chunk = 2 * MXU_TILE
    n_chunks = len(dilations) * D_MODEL // chunk

    def matmul(i):
        col = i * chunk
        w_ref = w_refs[0]
        if col >= w_ref.shape[1]:
            w_ref, col = w_refs[1], col - w_ref.shape[1]
        return jnp.dot(h_scr[...], w_ref[:, col:col + chunk],
                       preferred_element_type=F32)

    def epilogue(i, y):
        col = i * chunk
        out_ref, r = out_refs[col // D_MODEL], dilations[col // D_MODEL]
        slabs = [y[:, s * LANES:(s + 1) * LANES] for s in range(chunk // LANES)]
        if col < n_rope:
            sq = jnp.concatenate([slabs[0] * slabs[0] + slabs[1] * slabs[1],
                                  slabs[2] * slabs[2] + slabs[3] * slabs[3]], axis=1)
            ss = jnp.dot(sq.astype(BF16), gm_ref[...], preferred_element_type=F32)
            rs = lax.rsqrt(ss + EPS)
            sec = col // D_MODEL
            rot = []
            for t in range(2):
                rs_t = rs[:, t * LANES:(t + 1) * LANES]
                y0, y1 = slabs[2 * t], slabs[2 * t + 1]
                rot += [(y0 * rot_ref[sec, 0] - y1 * rot_ref[sec, 1]) * rs_t,
                        (y1 * rot_ref[sec, 2] + y0 * rot_ref[sec, 3]) * rs_t]
            slabs = rot
        for s, slab_val in enumerate(slabs):
            lo = col % D_MODEL + s * LANES
            if r == HEAD_BLOCKED:
                width = out_ref.shape[-1]
                out_ref[0, lo // width, :, lo % width:lo % width + LANES] = (
                    slab_val.astype(out_ref.dtype))
            elif r == 1:
                out_ref[:, lo:lo + LANES] = slab_val.astype(out_ref.dtype)
            else:
                slab = (i * (chunk // LANES) + s) % slab_scr.shape[0]
                slab_scr[slab] = slab_val
                if r <= SINGLE_OP_STRIDE:
                    for cls in range(r):
                        out_ref[:, cls * D_MODEL + lo:cls * D_MODEL + lo + LANES] = (
                            slab_scr[slab, pl.ds(cls, tm // r, stride=r), :]
                            .astype(out_ref.dtype))
                else:
                    r1 = SINGLE_OP_STRIDE
                    r2 = r // r1
                    rows1 = tm // r1
                    for c1 in range(r1):
                        part_scr[slab, c1 * rows1:(c1 + 1) * rows1, :] = (
                            slab_scr[slab, pl.ds(c1, rows1, stride=r1), :])
                    for c1 in range(r1):
                        for c2 in range(r2):
                            cls = c1 + r1 * c2
                            out_ref[:, cls * D_MODEL + lo:cls * D_MODEL + lo + LANES] = (
                                part_scr[slab, pl.ds(c1 * rows1 + c2, tm // r, stride=r2), :]
                                .astype(out_ref.dtype))

    for i in range(n_chunks):
        epilogue(i, matmul(i))


def _layer_block(stack, layer):
    return pl.BlockSpec((None,) + stack.shape[1:], lambda i: (layer, 0, 0),
                        pipeline_mode=pl.Buffered(1))


def _project(x2, gain, weights, layer, rot_tables, gmat, dilations, seq, tm=512):
    n = x2.shape[0]
    n_sec = rot_tables.shape[0]
    assert weights[0].shape[2] == n_sec * D_MODEL
    pos_blocks = seq // tm
    kern = functools.partial(_proj_kernel, n_rope=n_sec * D_MODEL, n_weights=len(weights),
                             dilations=dilations, tm=tm)
    out_specs, out_shape = [], []
    for sec, r in enumerate(dilations):
        if r == HEAD_BLOCKED:
            width = MXU_TILE if sec < n_sec else LANES
            out_specs.append(pl.BlockSpec((1, D_MODEL // width, tm, width),
                                          lambda i: (i // pos_blocks, 0, i % pos_blocks, 0)))
            out_shape.append(jax.ShapeDtypeStruct((n // seq, D_MODEL // width, seq, width), BF16))
        else:
            out_specs.append(pl.BlockSpec((tm // r, r * D_MODEL), lambda i: (i, 0)))
            out_shape.append(jax.ShapeDtypeStruct((n // r, r * D_MODEL), BF16))
    return pl.pallas_call(
        kern,
        grid=(n // tm,),
        in_specs=[
            pl.BlockSpec((tm, D_MODEL), lambda i: (i, 0)),
            pl.BlockSpec((1, D_MODEL), lambda i: (0, 0)),
            pl.BlockSpec((n_sec, 4, tm, LANES), lambda i: (0, 0, i % pos_blocks, 0)),
            pl.BlockSpec((MXU_TILE, MXU_TILE), lambda i: (0, 0)),
            *[_layer_block(w, layer) for w in weights],
        ],
        out_specs=out_specs,
        out_shape=out_shape,
        scratch_shapes=[pltpu.VMEM((8, tm, LANES), F32), pltpu.VMEM((8, tm, LANES), F32),
                        pltpu.VMEM((tm, D_MODEL), BF16)],
        compiler_params=_params(1),
        name="proj",
    )(x2, gain.reshape(1, D_MODEL), rot_tables, gmat, *weights)


def _diff_attn_kernel(lq1_ref, lk1_ref, lq2_ref, lk2_ref, sg_ref,
                      q_ref, k_ref, v_ref, o_ref, s_scr, p_scr, linv_scr,
                      *, tq, seq, lam_init, row_chunk):
    lam = (jnp.exp(jnp.sum(lq1_ref[...] * lk1_ref[...], axis=-1, keepdims=True))
           - jnp.exp(jnp.sum(lq2_ref[...] * lk2_ref[...], axis=-1, keepdims=True))
           + lam_init)
    owner = _head_of_lane((tq, MXU_TILE), 1)
    rr = lax.broadcasted_iota(jnp.int32, (row_chunk, LANES), 0)
    cc = lax.broadcasted_iota(jnp.int32, (row_chunk, LANES), 1)
    delta = rr - cc

    for qi in reversed(range(seq // tq)):
        buf = qi % 2
        kc = (qi + 1) * tq
        q = q_ref[qi * tq:(qi + 1) * tq, :]
        zero = jnp.zeros_like(q)
        q4 = jnp.concatenate([jnp.where(owner == j, q, zero)
                              for j in range(HEADS_PER_TILE)], axis=0)
        s_scr[buf, :, :kc] = lax.dot_general(
            q4, k_ref[:kc, :], (((1,), (1,)), ((), ())),
            preferred_element_type=F32)

        for r in range(HEADS_PER_TILE * tq // row_chunk):
            rows = slice(r * row_chunk, (r + 1) * row_chunk)
            row0 = (r * row_chunk) % tq

            def tile(c):
                t = s_scr[buf, rows, c * LANES:(c + 1) * LANES]
                col0 = c * LANES - qi * tq
                if col0 + LANES - 1 > row0:
                    t = jnp.where(delta >= col0 - row0, t, NEG_BIG)
                return t

            tiles = [tile(c) for c in range(kc // LANES)]
            m = tiles[0]
            for t in tiles[1:]:
                m = jnp.maximum(m, t)
            mrow = jnp.max(m, axis=-1, keepdims=True)
            lsum = None
            for c, t in enumerate(tiles):
                p = jnp.exp2(t - mrow)
                lsum = p if lsum is None else lsum + p
                p_scr[buf, rows, c * LANES:(c + 1) * LANES] = p.astype(BF16)
            l = jnp.sum(lsum, axis=-1, keepdims=True)
            linv_scr[buf, rows, :] = jnp.broadcast_to(1.0 / l, (row_chunk, LANES))

        for hh in range(HEADS_PER_TILE // 2):
            rows = slice(hh * 2 * tq, (hh + 1) * 2 * tq)
            acc = jnp.dot(p_scr[buf, rows, :kc], v_ref[hh, :kc, :],
                          preferred_element_type=F32)
            acc = acc * linv_scr[buf, rows]
            o = acc[:tq] - lam * acc[tq:]
            ms = jnp.mean(o * o, axis=-1, keepdims=True)
            o = o * lax.rsqrt(ms + EPS) * sg_ref[...] * (1.0 - lam_init)
            o_ref[hh, qi * tq:(qi + 1) * tq, :] = o.astype(o_ref.dtype)


def _diff_attention(q, k, v, lam_vecs, sub_gain, lam_init, tq=256, row_chunk=64):
    batch, heads, seq, _ = v.shape
    pair = HEADS_PER_TILE // 2
    rows = HEADS_PER_TILE * tq
    vec = pl.BlockSpec((1, HEAD_DIM), lambda b, t: (0, 0))
    qk_spec = pl.BlockSpec((None, None, seq, MXU_TILE), lambda b, t: (b, t, 0, 0))
    head = pl.BlockSpec((None, pair, seq, LANES), lambda b, t: (b, t, 0, 0))
    kern = functools.partial(_diff_attn_kernel, tq=tq, seq=seq, lam_init=lam_init,
                             row_chunk=row_chunk)
    return pl.pallas_call(
        kern,
        grid=(batch, heads // pair),
        in_specs=[vec, vec, vec, vec,
                  pl.BlockSpec((1, 2 * HEAD_DIM), lambda b, t: (0, 0)),
                  qk_spec, qk_spec, head],
        out_specs=head,
        out_shape=jax.ShapeDtypeStruct(v.shape, BF16),
        scratch_shapes=[pltpu.VMEM((2, rows, seq), F32),
                        pltpu.VMEM((2, rows, seq), BF16),
                        pltpu.VMEM((2, rows, LANES), F32)],
        compiler_params=_params(2),
        name="diff_attn",
    )(*[u.reshape(1, HEAD_DIM) for u in lam_vecs], sub_gain.reshape(1, 2 * HEAD_DIM),
      q, k, v)


def _dilated_kernel(q_ref, k_ref, v_ref, o_ref, max_ref, den_ref, *, tl, has_prev, classes):
    li = pl.program_id(2)
    nqb = tl // BLOCK
    nkeys = 2 * BLOCK if has_prev else BLOCK
    owner = _head_of_lane((BLOCK, MXU_TILE), 1)
    vhead = lax.broadcasted_iota(jnp.int32, (BLOCK, MXU_TILE), 1) // HEAD_DIM
    row = lax.broadcasted_iota(jnp.int32, (BLOCK, nkeys), 0)
    colm = lax.broadcasted_iota(jnp.int32, (BLOCK, nkeys), 1)
    delta = row - colm

    for qb in range(nqb):
        gb = li * nqb + qb
        rows_q = slice(qb * BLOCK, (qb + 1) * BLOCK)
        if has_prev:
            first = jnp.maximum(gb - 1, 0)
            k0 = pl.multiple_of(first * BLOCK, BLOCK)
            dist = delta + (gb - first) * BLOCK
            valid = (dist >= 0) & (dist <= BLOCK)
        else:
            k0 = pl.multiple_of(gb * BLOCK, BLOCK)
            valid = delta >= 0
        bias = jnp.where(valid, 0.0, NEG_BIG)
        for cl in range(classes):
            stat_cols = slice(cl * LANES, (cl + 1) * LANES)
            max_ref[0, rows_q, stat_cols] = jnp.zeros((BLOCK, LANES), F32)
            den_ref[0, rows_q, stat_cols] = jnp.ones((BLOCK, LANES), F32)
            for t in range(D_MODEL // MXU_TILE):
                lo = cl * D_MODEL + t * MXU_TILE
                cs = slice(lo, lo + MXU_TILE)
                q = q_ref[0, rows_q, cs]
                kb = k_ref[0, pl.ds(k0, nkeys), cs]
                vb = v_ref[0, pl.ds(k0, nkeys), cs]
                zero = jnp.zeros_like(q)
                q4 = jnp.concatenate([jnp.where(owner == j, q, zero)
                                      for j in range(HEADS_PER_TILE)], axis=0)
                s = lax.dot_general(q4, kb, (((1,), (1,)), ((), ())),
                                    preferred_element_type=F32)
                s = (s.reshape(HEADS_PER_TILE, BLOCK, nkeys) + bias[None]
                     ).reshape(HEADS_PER_TILE * BLOCK, nkeys)
                m = jnp.max(s, axis=-1, keepdims=True)
                p = jnp.exp2(s - m)
                den = jnp.sum(p, axis=-1, keepdims=True)
                pv = jnp.dot(p.astype(BF16), vb, preferred_element_type=F32)
                o = pv[:BLOCK]
                for j in range(HEADS_PER_TILE):
                    rows = slice(j * BLOCK, (j + 1) * BLOCK)
                    col = cl * LANES + HEADS_PER_TILE * t + j
                    if j:
                        o = jnp.where(vhead == j, pv[rows], o)
                    max_ref[0, rows_q, col:col + 1] = m[rows]
                    den_ref[0, rows_q, col:col + 1] = den[rows]
                o_ref[0, rows_q, cs] = o.astype(o_ref.dtype)


def _dilated_group(q_g, k_g, v_g, g, batch, seq):
    _, r = DIL_GROUPS[g]
    sub_len = seq // r
    tl = min(sub_len, 512)
    classes = min(r, 512 // tl)
    view = (batch, sub_len, r * D_MODEL)
    kern = functools.partial(_dilated_kernel, tl=tl, has_prev=sub_len > BLOCK,
                             classes=classes)
    width = classes * D_MODEL
    stat_spec = pl.BlockSpec((1, tl, classes * LANES), lambda b, c, i: (b, i, c))
    stat_shape = jax.ShapeDtypeStruct((batch, sub_len, r * LANES), F32)
    o, mx, den = pl.pallas_call(
        kern,
        grid=(batch, r // classes, sub_len // tl),
        in_specs=[
            pl.BlockSpec((1, tl, width), lambda b, c, i: (b, i, c)),
            pl.BlockSpec((1, sub_len, width), lambda b, c, i: (b, 0, c)),
            pl.BlockSpec((1, sub_len, width), lambda b, c, i: (b, 0, c)),
        ],
        out_specs=[pl.BlockSpec((1, tl, width), lambda b, c, i: (b, i, c)),
                   stat_spec, stat_spec],
        out_shape=[jax.ShapeDtypeStruct((batch, sub_len, r * D_MODEL), BF16),
                   stat_shape, stat_shape],
        compiler_params=_params(3),
        name=f"dilated_g{g}",
    )(q_g.reshape(view), k_g.reshape(view), v_g.reshape(view))
    n = batch * seq
    return (o.reshape(n // r, r * D_MODEL), mx.reshape(n // r, r * LANES),
            den.reshape(n // r, r * LANES))


FF_CHUNK = 512


def _mlp_into(out_ref, x, g_ref, wu_ref, wd_ref):
    ms = jnp.mean(x * x, axis=-1, keepdims=True)
    h = (x * lax.rsqrt(ms + EPS) * g_ref[...]).astype(BF16)
    out_ref[...] = x
    for c in range(D_FF // FF_CHUNK):
        cs = slice(c * FF_CHUNK, (c + 1) * FF_CHUNK)
        u = jnp.dot(h, wu_ref[:, cs], preferred_element_type=F32)
        a = jnp.square(jnp.maximum(u, 0.0)).astype(BF16)
        out_ref[...] += jnp.dot(a, wd_ref[cs, :], preferred_element_type=F32)


def _attn_out_mlp_kernel(x_ref, att_ref, wo_ref, g_ref, wu_ref, wd_ref, out_ref):
    att = jnp.concatenate([att_ref[h] for h in range(att_ref.shape[0])], axis=1)
    x1 = x_ref[...] + jnp.dot(att, wo_ref[...], preferred_element_type=F32)
    _mlp_into(out_ref, x1, g_ref, wu_ref, wd_ref)


def _mix_out_mlp_kernel(x_ref, o0_ref, o1_ref, o2_ref, m0_ref, m1_ref, m2_ref,
                        d0_ref, d1_ref, d2_ref, e_ref, wo_ref, g_ref, wu_ref, wd_ref,
                        out_ref, o_scr, st_scr, *, tm):
    n_slabs = D_MODEL // LANES
    for g, (o_ref, m_ref, d_ref) in enumerate(((o1_ref, m1_ref, d1_ref),
                                               (o2_ref, m2_ref, d2_ref))):
        r = DIL_GROUPS[g + 1][1]
        for cls in range(r):
            dst = pl.ds(cls, tm // r, stride=r)
            st_scr[0, g, dst, :] = m_ref[:, cls * LANES:(cls + 1) * LANES]
            st_scr[1, g, dst, :] = d_ref[:, cls * LANES:(cls + 1) * LANES]
            for s in range(n_slabs):
                lo = cls * D_MODEL + s * LANES
                o_scr[g, s, dst, :] = o_ref[:, lo:lo + LANES].astype(F32)

    maxes = (m0_ref[...], st_scr[0, 0], st_scr[0, 1])
    dens = (d0_ref[...], st_scr[1, 0], st_scr[1, 1])
    top = jnp.maximum(jnp.maximum(maxes[0], maxes[1]), maxes[2])
    es = [jnp.exp2(m - top) for m in maxes]
    inv = 1.0 / (es[0] * dens[0] + es[1] * dens[1] + es[2] * dens[2])
    wexp = []
    for e in es:
        wg = e * inv
        hi = wg.astype(BF16)
        lo = (wg - hi.astype(F32)).astype(BF16)
        wexp.append(jnp.dot(jnp.concatenate([hi, lo], axis=1), e_ref[...],
                            preferred_element_type=F32))
    slabs = []
    for s in range(n_slabs):
        cs = slice(s * LANES, (s + 1) * LANES)
        mixed = (wexp[0][:, cs] * o0_ref[:, cs].astype(F32)
                 + wexp[1][:, cs] * o_scr[0, s] + wexp[2][:, cs] * o_scr[1, s])
        slabs.append(mixed.astype(BF16))
    x1 = x_ref[...] + jnp.dot(jnp.concatenate(slabs, axis=1), wo_ref[...],
                              preferred_element_type=F32)
    _mlp_into(out_ref, x1, g_ref, wu_ref, wd_ref)


def _resident(shape):
    return pl.BlockSpec(shape, lambda i: (0,) * len(shape), pipeline_mode=pl.Buffered(1))


def _attn_out_mlp(x2, att, w_o, attn_layer, gain, w_up, w_down, layer, tm=1024):
    n = x2.shape[0]
    _, heads, seq, width = att.shape
    pos_blocks = seq // tm
    row = pl.BlockSpec((tm, D_MODEL), lambda i: (i, 0))
    att_spec = pl.BlockSpec((None, heads, tm, width),
                            lambda i: (i // pos_blocks, 0, i % pos_blocks, 0))
    return pl.pallas_call(
        _attn_out_mlp_kernel,
        grid=(n // tm,),
        in_specs=[row, att_spec, _layer_block(w_o, attn_layer), _resident((1, D_MODEL)),
                  _layer_block(w_up, layer), _layer_block(w_down, layer)],
        out_specs=row,
        out_shape=jax.ShapeDtypeStruct((n, D_MODEL), F32),
        compiler_params=_params(1),
        name="attn_out_mlp",
    )(x2, att, w_o, gain.reshape(1, D_MODEL), w_up, w_down)


def _mix_out_mlp(x2, outs, maxes, dens, expand, w_o, attn_layer, gain, w_up, w_down, layer,
                 tm=512):
    n = x2.shape[0]
    row = pl.BlockSpec((tm, D_MODEL), lambda i: (i, 0))
    o_specs = [pl.BlockSpec((tm // r, r * D_MODEL), lambda i: (i, 0)) for _, r in DIL_GROUPS]
    st_specs = [pl.BlockSpec((tm // r, r * LANES), lambda i: (i, 0)) for _, r in DIL_GROUPS]
    return pl.pallas_call(
        functools.partial(_mix_out_mlp_kernel, tm=tm),
        grid=(n // tm,),
        in_specs=[row, *o_specs, *st_specs, *st_specs,
                  _resident((2 * LANES, D_MODEL)), _layer_block(w_o, attn_layer),
                  _resident((1, D_MODEL)), _layer_block(w_up, layer),
                  _layer_block(w_down, layer)],
        out_specs=row,
        out_shape=jax.ShapeDtypeStruct((n, D_MODEL), F32),
        scratch_shapes=[pltpu.VMEM((N_GROUPS - 1, D_MODEL // LANES, tm, LANES), F32),
                        pltpu.VMEM((2, N_GROUPS - 1, tm, LANES), F32)],
        compiler_params=_params(1),
        name="mix_out_mlp",
    )(x2, *outs, *maxes, *dens, expand, w_o, gain.reshape(1, D_MODEL), w_up, w_down)


def _rope_tables(seq):
    inv = 1.0 / (ROPE_THETA ** (jnp.arange(0, HEAD_DIM, 2, dtype=F32) / HEAD_DIM))
    ang = jnp.arange(seq, dtype=F32)[:, None] * inv[None, :]
    planes = jnp.stack([jnp.cos(ang), jnp.sin(ang), jnp.cos(ang), jnp.sin(ang)])
    return jnp.tile(planes, (1, 1, LANES // HALF_DIM))


def _rotary_gain_tables(rope, gains):
    g_lo, g_hi = gains[:, :HALF_DIM], gains[:, HALF_DIM:]
    g = jnp.tile(jnp.stack([g_lo, g_hi, g_hi, g_lo], axis=1), (1, 1, LANES // HALF_DIM))
    return g[:, :, None, :] * rope[None]


def _weight_prep_kernel(w_ref, perm_ref, *out_refs, n_rot):
    for t in range(n_rot // MXU_TILE):
        cs = slice(t * MXU_TILE, (t + 1) * MXU_TILE)
        out_refs[0][:, cs] = jnp.dot(w_ref[:, cs].astype(BF16), perm_ref[...],
                                     preferred_element_type=F32).astype(BF16)
    if len(out_refs) > 1:
        out_refs[1][...] = w_ref[:, n_rot:].astype(BF16)


def _prepare_weights(w, n_rot, rows_per_step=256):
    layers, rows, cols = w.shape
    new = jnp.arange(MXU_TILE)
    old = ((new % LANES) // HALF_DIM) * HEAD_DIM + (new // LANES) * HALF_DIM + new % HALF_DIM
    perm = (jnp.arange(MXU_TILE)[:, None] == old[None, :]).astype(BF16)
    widths = (n_rot,) + ((cols - n_rot,) if cols > n_rot else ())
    outs = pl.pallas_call(
        functools.partial(_weight_prep_kernel, n_rot=n_rot),
        grid=(layers, rows // rows_per_step),
        in_specs=[pl.BlockSpec((None, rows_per_step, cols), lambda l, i: (l, i, 0)),
                  pl.BlockSpec((MXU_TILE, MXU_TILE), lambda l, i: (0, 0))],
        out_specs=[pl.BlockSpec((None, rows_per_step, wd), lambda l, i: (l, i, 0))
                   for wd in widths],
        out_shape=[jax.ShapeDtypeStruct((layers, rows, wd), BF16) for wd in widths],
        compiler_params=_params(2),
        name="weight_prep",
    )(w, perm)
    return tuple(outs)


def _head_mean_matrix():
    i = jnp.arange(MXU_TILE)
    same = i[:, None] // HALF_DIM == i[None, :] // HALF_DIM
    return (same.astype(F32) / HEAD_DIM).astype(BF16)


def _head_expand_matrix():
    r = jnp.arange(2 * LANES)[:, None] % LANES
    c = jnp.arange(D_MODEL)[None, :] // HEAD_DIM
    return (r == c).astype(BF16)


def kernel(x, a_norm, a_w_qkv, a_q_gain, a_k_gain, a_lam_q1, a_lam_k1, a_lam_q2, a_lam_k2, a_sub_gain, a_w_o, kv_norm, kv_w, kv_k_gain, b_norm, b_w_q, b_q_gain, b_w_o, m_norm, m_w_up, m_w_down):
    batch, seq, _ = x.shape
    n = batch * seq
    scale = HEAD_DIM ** -0.5
    rope = _rope_tables(seq)
    gmat = _head_mean_matrix()
    expand = _head_expand_matrix()
    x2 = x.reshape(n, D_MODEL)
    gw = N_GROUPS * D_MODEL
    dil = tuple(r for _, r in DIL_GROUPS)

    a_w = _prepare_weights(a_w_qkv, 2 * D_MODEL)
    kv_weights = _prepare_weights(kv_w[None], gw)
    b_w = _prepare_weights(b_w_q, gw)
    a_wo, b_wo = a_w_o.astype(BF16), b_w_o.astype(BF16)
    w_up, w_down = m_w_up.astype(BF16), m_w_down.astype(BF16)

    k_sh = v_sh = None
    for layer in range(DEPTH):
        if layer < N_A_LAYERS:
            lam_init = 0.8 - 0.6 * math.exp(-0.3 * layer)
            tables = _rotary_gain_tables(
                rope, jnp.stack([a_q_gain[layer] * (scale * LOG2_E), a_k_gain[layer]]))
            q, k, v = _project(x2, a_norm[layer], a_w, layer, tables, gmat,
                               (HEAD_BLOCKED,) * 3, seq)
            att = _diff_attention(
                q, k, v,
                (a_lam_q1[layer], a_lam_k1[layer], a_lam_q2[layer], a_lam_k2[layer]),
                a_sub_gain[layer], lam_init)
            x2 = _attn_out_mlp(x2, att, a_wo, layer, m_norm[layer], w_up, w_down, layer)
        else:
            if layer == N_A_LAYERS:
                kv = _project(x2, kv_norm, kv_weights, 0, _rotary_gain_tables(rope, kv_k_gain),
                              gmat, dil + dil, seq)
                k_sh, v_sh = kv[:N_GROUPS], kv[N_GROUPS:]
            bl = layer - N_A_LAYERS
            tables = _rotary_gain_tables(rope, b_q_gain[bl] * (scale * LOG2_E))
            qs = _project(x2, b_norm[bl], b_w, bl, tables, gmat, dil, seq)
            stats = [_dilated_group(qs[g], k_sh[g], v_sh[g], g, batch, seq)
                     for g in range(N_GROUPS)]
            outs, maxes, dens = zip(*stats)
            x2 = _mix_out_mlp(x2, outs, maxes, dens, expand, b_wo, bl, m_norm[layer],
                              w_up, w_down, layer)
    return x2.reshape(batch, seq, D_MODEL)
```

```python
import functools
import math

import jax
import jax.numpy as jnp
from jax import lax
from jax.experimental import pallas as pl
from jax.experimental.pallas import tpu as pltpu

D_MODEL = 1024
HEAD_DIM = 64
HALF_DIM = HEAD_DIM // 2
DEPTH = 4
N_A_LAYERS = DEPTH // 2
DIFF_HEADS = D_MODEL // (2 * HEAD_DIM)
DIL_GROUPS = ((128, 1), (512, 4), (2048, 16))
N_GROUPS = len(DIL_GROUPS)
DIL_HEADS = D_MODEL // HEAD_DIM
D_FF = 4 * D_MODEL
ROPE_THETA = 10000.0
BLOCK = 128
EPS = 1e-6

LANES = 128
MXU_TILE = 256
HEADS_PER_TILE = MXU_TILE // HEAD_DIM
HEAD_BLOCKED = 0
SINGLE_OP_STRIDE = 4
VMEM_LIMIT = 56 * 1024 * 1024
NEG_BIG = -1e30
LOG2_E = math.log2(math.e)

F32 = jnp.float32
BF16 = jnp.bfloat16


def _params(n_axes):
    return pltpu.CompilerParams(
        dimension_semantics=("arbitrary",) * n_axes,
        vmem_limit_bytes=VMEM_LIMIT)


def _head_of_lane(shape, axis):
    lane = lax.broadcasted_iota(jnp.int32, shape, axis)
    return (lane % LANES) // HALF_DIM


def _proj_kernel(x_ref, g_ref, rot_ref, gm_ref, *refs, n_rope, n_weights, dilations, tm):
    w_refs, refs = refs[:n_weights], refs[n_weights:]
    out_refs = refs[:len(dilations)]
    slab_scr, part_scr, h_scr = refs[len(dilations):]
    x = x_ref[...]
    ms = jnp.mean(x * x, axis=-1, keepdims=True)
    h_scr[...] = (x * lax.rsqrt(ms + EPS) * g_ref[...]).astype(BF16)
    chunk = 2 * MXU_TILE
    n_chunks = len(dilations) * D_MODEL // chunk

    def matmul(i):
        col = i * chunk
        w_ref = w_refs[0]
        if col >= w_ref.shape[1]:
            w_ref, col = w_refs[1], col - w_ref.shape[1]
        return jnp.dot(h_scr[...], w_ref[:, col:col + chunk],
                       preferred_element_type=F32)

    def epilogue(i, y):
        col = i * chunk
        out_ref, r = out_refs[col // D_MODEL], dilations[col // D_MODEL]
        slabs = [y[:, s * LANES:(s + 1) * LANES] for s in range(chunk // LANES)]
        if col < n_rope:
            sq = jnp.concatenate([slabs[0] * slabs[0] + slabs[1] * slabs[1],
                                  slabs[2] * slabs[2] + slabs[3] * slabs[3]], axis=1)
            ss = jnp.dot(sq.astype(BF16), gm_ref[...], preferred_element_type=F32)
            rs = lax.rsqrt(ss + EPS)
            sec = col // D_MODEL
            rot = []
            for t in range(2):
                rs_t = rs[:, t * LANES:(t + 1) * LANES]
                y0, y1 = slabs[2 * t], slabs[2 * t + 1]
                rot += [(y0 * rot_ref[sec, 0] - y1 * rot_ref[sec, 1]) * rs_t,
                        (y1 * rot_ref[sec, 2] + y0 * rot_ref[sec, 3]) * rs_t]
            slabs = rot
        for s, slab_val in enumerate(slabs):
            lo = col % D_MODEL + s * LANES
            if r == HEAD_BLOCKED:
                width = out_ref.shape[-1]
                out_ref[0, lo // width, :, lo % width:lo % width + LANES] = (
                    slab_val.astype(out_ref.dtype))
            elif r == 1:
                out_ref[:, lo:lo + LANES] = slab_val.astype(out_ref.dtype)
            else:
                slab = (i * (chunk // LANES) + s) % slab_scr.shape[0]
                slab_scr[slab] = slab_val
                if r <= SINGLE_OP_STRIDE:
                    for cls in range(r):
                        out_ref[:, cls * D_MODEL + lo:cls * D_MODEL + lo + LANES] = (
                            slab_scr[slab, pl.ds(cls, tm // r, stride=r), :]
                            .astype(out_ref.dtype))
                else:
                    r1 = SINGLE_OP_STRIDE
                    r2 = r // r1
                    rows1 = tm // r1
                    for c1 in range(r1):
                        part_scr[slab, c1 * rows1:(c1 + 1) * rows1, :] = (
                            slab_scr[slab, pl.ds(c1, rows1, stride=r1), :])
                    for c1 in range(r1):
                        for c2 in range(r2):
                            cls = c1 + r1 * c2
                            out_ref[:, cls * D_MODEL + lo:cls * D_MODEL + lo + LANES] = (
                                part_scr[slab, pl.ds(c1 * rows1 + c2, tm // r, stride=r2), :]
                                .astype(out_ref.dtype))

    for i in range(n_chunks):
        epilogue(i, matmul(i))


def _layer_block(stack, layer):
    return pl.BlockSpec((None,) + stack.shape[1:], lambda i: (layer, 0, 0),
                        pipeline_mode=pl.Buffered(1))


SPILL_ALLOWANCE = 6 * 1024 * 1024


def _proj_row_tile(n_sections, n_rotary):
    for tm in (1024, 512):
        per_row = (2 * D_MODEL * 4
                   + 2 * n_sections * D_MODEL * 2
                   + 16 * LANES * 4 + D_MODEL * 2
                   + 2 * n_rotary * 4 * LANES * 4)
        resident = n_sections * D_MODEL * D_MODEL * 2
        if tm * per_row + resident + SPILL_ALLOWANCE <= VMEM_LIMIT:
            return tm
    raise ValueError("projection does not fit VMEM")


def _project(x2, gain, weights, layer, rot_tables, gmat, dilations, seq):
    n = x2.shape[0]
    n_sec = rot_tables.shape[0]
    assert weights[0].shape[2] == n_sec * D_MODEL
    tm = _proj_row_tile(len(dilations), n_sec)
    pos_blocks = seq // tm
    kern = functools.partial(_proj_kernel, n_rope=n_sec * D_MODEL, n_weights=len(weights),
                             dilations=dilations, tm=tm)
    out_specs, out_shape = [], []
    for sec, r in enumerate(dilations):
        if r == HEAD_BLOCKED:
            width = MXU_TILE if sec < n_sec else LANES
            out_specs.append(pl.BlockSpec((1, D_MODEL // width, tm, width),
                                          lambda i: (i // pos_blocks, 0, i % pos_blocks, 0)))
            out_shape.append(jax.ShapeDtypeStruct((n // seq, D_MODEL // width, seq, width), BF16))
        else:
            out_specs.append(pl.BlockSpec((tm // r, r * D_MODEL), lambda i: (i, 0)))
            out_shape.append(jax.ShapeDtypeStruct((n // r, r * D_MODEL), BF16))
    return pl.pallas_call(
        kern,
        grid=(n // tm,),
        in_specs=[
            pl.BlockSpec((tm, D_MODEL), lambda i: (i, 0)),
            pl.BlockSpec((1, D_MODEL), lambda i: (0, 0)),
            pl.BlockSpec((n_sec, 4, tm, LANES), lambda i: (0, 0, i % pos_blocks, 0)),
            pl.BlockSpec((MXU_TILE, MXU_TILE), lambda i: (0, 0)),
            *[_layer_block(w, layer) for w in weights],
        ],
        out_specs=out_specs,
        out_shape=out_shape,
        scratch_shapes=[pltpu.VMEM((8, tm, LANES), F32), pltpu.VMEM((8, tm, LANES), F32),
                        pltpu.VMEM((tm, D_MODEL), BF16)],
        compiler_params=_params(1),
        name="proj",
    )(x2, gain.reshape(1, D_MODEL), rot_tables, gmat, *weights)


def _diff_attn_kernel(lq1_ref, lk1_ref, lq2_ref, lk2_ref, sg_ref,
                      q_ref, k_ref, v_ref, o_ref, s_scr, p_scr, linv_scr,
                      *, tq, seq, lam_init, row_chunk):
    lam = (jnp.exp(jnp.sum(lq1_ref[...] * lk1_ref[...], axis=-1, keepdims=True))
           - jnp.exp(jnp.sum(lq2_ref[...] * lk2_ref[...], axis=-1, keepdims=True))
           + lam_init)
    owner = _head_of_lane((tq, MXU_TILE), 1)
    rr = lax.broadcasted_iota(jnp.int32, (row_chunk, LANES), 0)
    cc = lax.broadcasted_iota(jnp.int32, (row_chunk, LANES), 1)
    delta = rr - cc

    for qi in reversed(range(seq // tq)):
        buf = qi % 2
        kc = (qi + 1) * tq
        q = q_ref[qi * tq:(qi + 1) * tq, :]
        zero = jnp.zeros_like(q)
        q4 = jnp.concatenate([jnp.where(owner == j, q, zero)
                              for j in range(HEADS_PER_TILE)], axis=0)
        s_scr[buf, :, :kc] = lax.dot_general(
            q4, k_ref[:kc, :], (((1,), (1,)), ((), ())),
            preferred_element_type=F32)

        for r in range(HEADS_PER_TILE * tq // row_chunk):
            rows = slice(r * row_chunk, (r + 1) * row_chunk)
            row0 = (r * row_chunk) % tq

            def tile(c):
                t = s_scr[buf, rows, c * LANES:(c + 1) * LANES]
                col0 = c * LANES - qi * tq
                if col0 + LANES - 1 > row0:
                    t = jnp.where(delta >= col0 - row0, t, NEG_BIG)
                return t

            tiles = [tile(c) for c in range(kc // LANES)]
            m = tiles[0]
            for t in tiles[1:]:
                m = jnp.maximum(m, t)
            mrow = jnp.max(m, axis=-1, keepdims=True)
            lsum = None
            for c, t in enumerate(tiles):
                p = jnp.exp2(t - mrow)
                lsum = p if lsum is None else lsum + p
                p_scr[buf, rows, c * LANES:(c + 1) * LANES] = p.astype(BF16)
            l = jnp.sum(lsum, axis=-1, keepdims=True)
            linv_scr[buf, rows, :] = jnp.broadcast_to(1.0 / l, (row_chunk, LANES))

        for hh in range(HEADS_PER_TILE // 2):
            rows = slice(hh * 2 * tq, (hh + 1) * 2 * tq)
            acc = jnp.dot(p_scr[buf, rows, :kc], v_ref[hh, :kc, :],
                          preferred_element_type=F32)
            acc = acc * linv_scr[buf, rows]
            o = acc[:tq] - lam * acc[tq:]
            ms = jnp.mean(o * o, axis=-1, keepdims=True)
            o = o * lax.rsqrt(ms + EPS) * sg_ref[...] * (1.0 - lam_init)
            o_ref[hh, qi * tq:(qi + 1) * tq, :] = o.astype(o_ref.dtype)


def _diff_attention(q, k, v, lam_vecs, sub_gain, lam_init, tq=256, row_chunk=64):
    batch, heads, seq, _ = v.shape
    pair = HEADS_PER_TILE // 2
    rows = HEADS_PER_TILE * tq
    vec = pl.BlockSpec((1, HEAD_DIM), lambda b, t: (0, 0))
    qk_spec = pl.BlockSpec((None, None, seq, MXU_TILE), lambda b, t: (b, t, 0, 0))
    head = pl.BlockSpec((None, pair, seq, LANES), lambda b, t: (b, t, 0, 0))
    kern = functools.partial(_diff_attn_kernel, tq=tq, seq=seq, lam_init=lam_init,
                             row_chunk=row_chunk)
    return pl.pallas_call(
        kern,
        grid=(batch, heads // pair),
        in_specs=[vec, vec, vec, vec,
                  pl.BlockSpec((1, 2 * HEAD_DIM), lambda b, t: (0, 0)),
                  qk_spec, qk_spec, head],
        out_specs=head,
        out_shape=jax.ShapeDtypeStruct(v.shape, BF16),
        scratch_shapes=[pltpu.VMEM((2, rows, seq), F32),
                        pltpu.VMEM((2, rows, seq), BF16),
                        pltpu.VMEM((2, rows, LANES), F32)],
        compiler_params=_params(2),
        name="diff_attn",
    )(*[u.reshape(1, HEAD_DIM) for u in lam_vecs], sub_gain.reshape(1, 2 * HEAD_DIM),
      q, k, v)


def _dilated_kernel(q_ref, k_ref, v_ref, o_ref, max_ref, den_ref, *, tl, has_prev, classes):
    li = pl.program_id(2)
    nqb = tl // BLOCK
    nkeys = 2 * BLOCK if has_prev else BLOCK
    owner = _head_of_lane((BLOCK, MXU_TILE), 1)
    vhead = lax.broadcasted_iota(jnp.int32, (BLOCK, MXU_TILE), 1) // HEAD_DIM
    row = lax.broadcasted_iota(jnp.int32, (BLOCK, nkeys), 0)
    colm = lax.broadcasted_iota(jnp.int32, (BLOCK, nkeys), 1)
    delta = row - colm

    for qb in range(nqb):
        gb = li * nqb + qb
        rows_q = slice(qb * BLOCK, (qb + 1) * BLOCK)
        if has_prev:
            first = jnp.maximum(gb - 1, 0)
            k0 = pl.multiple_of(first * BLOCK, BLOCK)
            dist = delta + (gb - first) * BLOCK
            valid = (dist >= 0) & (dist <= BLOCK)
        else:
            k0 = pl.multiple_of(gb * BLOCK, BLOCK)
            valid = delta >= 0
        bias = jnp.where(valid, 0.0, NEG_BIG)
        for cl in range(classes):
            stat_cols = slice(cl * LANES, (cl + 1) * LANES)
            max_ref[0, rows_q, stat_cols] = jnp.zeros((BLOCK, LANES), F32)
            den_ref[0, rows_q, stat_cols] = jnp.ones((BLOCK, LANES), F32)
            for t in range(D_MODEL // MXU_TILE):
                lo = cl * D_MODEL + t * MXU_TILE
                cs = slice(lo, lo + MXU_TILE)
                q = q_ref[0, rows_q, cs]
                kb = k_ref[0, pl.ds(k0, nkeys), cs]
                vb = v_ref[0, pl.ds(k0, nkeys), cs]
                zero = jnp.zeros_like(q)
                q4 = jnp.concatenate([jnp.where(owner == j, q, zero)
                                      for j in range(HEADS_PER_TILE)], axis=0)
                s = lax.dot_general(q4, kb, (((1,), (1,)), ((), ())),
                                    preferred_element_type=F32)
                s = (s.reshape(HEADS_PER_TILE, BLOCK, nkeys) + bias[None]
                     ).reshape(HEADS_PER_TILE * BLOCK, nkeys)
                m = jnp.max(s, axis=-1, keepdims=True)
                p = jnp.exp2(s - m)
                den = jnp.sum(p, axis=-1, keepdims=True)
                pv = jnp.dot(p.astype(BF16), vb, preferred_element_type=F32)
                o = pv[:BLOCK]
                for j in range(HEADS_PER_TILE):
                    rows = slice(j * BLOCK, (j + 1) * BLOCK)
                    col = cl * LANES + HEADS_PER_TILE * t + j
                    if j:
                        o = jnp.where(vhead == j, pv[rows], o)
                    max_ref[0, rows_q, col:col + 1] = m[rows]
                    den_ref[0, rows_q, col:col + 1] = den[rows]
                o_ref[0, rows_q, cs] = o.astype(o_ref.dtype)


def _dilated_group(q_g, k_g, v_g, g, batch, seq):
    _, r = DIL_GROUPS[g]
    sub_len = seq // r
    tl = min(sub_len, 512)
    classes = min(r, 512 // tl)
    view = (batch, sub_len, r * D_MODEL)
    kern = functools.partial(_dilated_kernel, tl=tl, has_prev=sub_len > BLOCK,
                             classes=classes)
    width = classes * D_MODEL
    stat_spec = pl.BlockSpec((1, tl, classes * LANES), lambda b, c, i: (b, i, c))
    stat_shape = jax.ShapeDtypeStruct((batch, sub_len, r * LANES), F32)
    o, mx, den = pl.pallas_call(
        kern,
        grid=(batch, r // classes, sub_len // tl),
        in_specs=[
            pl.BlockSpec((1, tl, width), lambda b, c, i: (b, i, c)),
            pl.BlockSpec((1, sub_len, width), lambda b, c, i: (b, 0, c)),
            pl.BlockSpec((1, sub_len, width), lambda b, c, i: (b, 0, c)),
        ],
        out_specs=[pl.BlockSpec((1, tl, width), lambda b, c, i: (b, i, c)),
                   stat_spec, stat_spec],
        out_shape=[jax.ShapeDtypeStruct((batch, sub_len, r * D_MODEL), BF16),
                   stat_shape, stat_shape],
        compiler_params=_params(3),
        name=f"dilated_g{g}",
    )(q_g.reshape(view), k_g.reshape(view), v_g.reshape(view))
    n = batch * seq
    return (o.reshape(n // r, r * D_MODEL), mx.reshape(n // r, r * LANES),
            den.reshape(n // r, r * LANES))


FF_CHUNK = 512


def _mlp_into(out_ref, x, g_ref, wu_ref, wd_ref):
    ms = jnp.mean(x * x, axis=-1, keepdims=True)
    h = (x * lax.rsqrt(ms + EPS) * g_ref[...]).astype(BF16)
    out_ref[...] = x
    for c in range(D_FF // FF_CHUNK):
        cs = slice(c * FF_CHUNK, (c + 1) * FF_CHUNK)
        u = jnp.dot(h, wu_ref[:, cs], preferred_element_type=F32)
        a = jnp.square(jnp.maximum(u, 0.0)).astype(BF16)
        out_ref[...] += jnp.dot(a, wd_ref[cs, :], preferred_element_type=F32)


def _attn_out_mlp_kernel(x_ref, att_ref, wo_ref, g_ref, wu_ref, wd_ref, out_ref):
    att = jnp.concatenate([att_ref[h] for h in range(att_ref.shape[0])], axis=1)
    x1 = x_ref[...] + jnp.dot(att, wo_ref[...], preferred_element_type=F32)
    _mlp_into(out_ref, x1, g_ref, wu_ref, wd_ref)


def _mix_out_mlp_kernel(x_ref, o0_ref, o1_ref, o2_ref, m0_ref, m1_ref, m2_ref,
                        d0_ref, d1_ref, d2_ref, e_ref, wo_ref, g_ref, wu_ref, wd_ref,
                        out_ref, o_scr, st_scr, *, tm):
    n_slabs = D_MODEL // LANES
    for g, (o_ref, m_ref, d_ref) in enumerate(((o1_ref, m1_ref, d1_ref),
                                               (o2_ref, m2_ref, d2_ref))):
        r = DIL_GROUPS[g + 1][1]
        for cls in range(r):
            dst = pl.ds(cls, tm // r, stride=r)
            st_scr[0, g, dst, :] = m_ref[:, cls * LANES:(cls + 1) * LANES]
            st_scr[1, g, dst, :] = d_ref[:, cls * LANES:(cls + 1) * LANES]
            for s in range(n_slabs):
                lo = cls * D_MODEL + s * LANES
                o_scr[g, s, dst, :] = o_ref[:, lo:lo + LANES].astype(F32)

    maxes = (m0_ref[...], st_scr[0, 0], st_scr[0, 1])
    dens = (d0_ref[...], st_scr[1, 0], st_scr[1, 1])
    top = jnp.maximum(jnp.maximum(maxes[0], maxes[1]), maxes[2])
    es = [jnp.exp2(m - top) for m in maxes]
    inv = 1.0 / (es[0] * dens[0] + es[1] * dens[1] + es[2] * dens[2])
    wexp = []
    for e in es:
        wg = e * inv
        hi = wg.astype(BF16)
        lo = (wg - hi.astype(F32)).astype(BF16)
        wexp.append(jnp.dot(jnp.concatenate([hi, lo], axis=1), e_ref[...],
                            preferred_element_type=F32))
    slabs = []
    for s in range(n_slabs):
        cs = slice(s * LANES, (s + 1) * LANES)
        mixed = (wexp[0][:, cs] * o0_ref[:, cs].astype(F32)
                 + wexp[1][:, cs] * o_scr[0, s] + wexp[2][:, cs] * o_scr[1, s])
        slabs.append(mixed.astype(BF16))
    x1 = x_ref[...] + jnp.dot(jnp.concatenate(slabs, axis=1), wo_ref[...],
                              preferred_element_type=F32)
    _mlp_into(out_ref, x1, g_ref, wu_ref, wd_ref)


def _resident(shape):
    return pl.BlockSpec(shape, lambda i: (0,) * len(shape), pipeline_mode=pl.Buffered(1))


def _attn_out_mlp(x2, att, w_o, attn_layer, gain, w_up, w_down, layer, tm=1024):
    n = x2.shape[0]
    _, heads, seq, width = att.shape
    pos_blocks = seq // tm
    row = pl.BlockSpec((tm, D_MODEL), lambda i: (i, 0))
    att_spec = pl.BlockSpec((None, heads, tm, width),
                            lambda i: (i // pos_blocks, 0, i % pos_blocks, 0))
    return pl.pallas_call(
        _attn_out_mlp_kernel,
        grid=(n // tm,),
        in_specs=[row, att_spec, _layer_block(w_o, attn_layer), _resident((1, D_MODEL)),
                  _layer_block(w_up, layer), _layer_block(w_down, layer)],
        out_specs=row,
        out_shape=jax.ShapeDtypeStruct((n, D_MODEL), F32),
        compiler_params=_params(1),
        name="attn_out_mlp",
    )(x2, att, w_o, gain.reshape(1, D_MODEL), w_up, w_down)


def _mix_out_mlp(x2, outs, maxes, dens, expand, w_o, attn_layer, gain, w_up, w_down, layer,
                 tm=512):
    n = x2.shape[0]
    row = pl.BlockSpec((tm, D_MODEL), lambda i: (i, 0))
    o_specs = [pl.BlockSpec((tm // r, r * D_MODEL), lambda i: (i, 0)) for _, r in DIL_GROUPS]
    st_specs = [pl.BlockSpec((tm // r, r * LANES), lambda i: (i, 0)) for _, r in DIL_GROUPS]
    return pl.pallas_call(
        functools.partial(_mix_out_mlp_kernel, tm=tm),
        grid=(n // tm,),
        in_specs=[row, *o_specs, *st_specs, *st_specs,
                  _resident((2 * LANES, D_MODEL)), _layer_block(w_o, attn_layer),
                  _resident((1, D_MODEL)), _layer_block(w_up, layer),
                  _layer_block(w_down, layer)],
        out_specs=row,
        out_shape=jax.ShapeDtypeStruct((n, D_MODEL), F32),
        scratch_shapes=[pltpu.VMEM((N_GROUPS - 1, D_MODEL // LANES, tm, LANES), F32),
                        pltpu.VMEM((2, N_GROUPS - 1, tm, LANES), F32)],
        compiler_params=_params(1),
        name="mix_out_mlp",
    )(x2, *outs, *maxes, *dens, expand, w_o, gain.reshape(1, D_MODEL), w_up, w_down)


def _rope_tables(seq):
    inv = 1.0 / (ROPE_THETA ** (jnp.arange(0, HEAD_DIM, 2, dtype=F32) / HEAD_DIM))
    ang = jnp.arange(seq, dtype=F32)[:, None] * inv[None, :]
    planes = jnp.stack([jnp.cos(ang), jnp.sin(ang), jnp.cos(ang), jnp.sin(ang)])
    return jnp.tile(planes, (1, 1, LANES // HALF_DIM))


def _rotary_gain_tables(rope, gains):
    g_lo, g_hi = gains[:, :HALF_DIM], gains[:, HALF_DIM:]
    g = jnp.tile(jnp.stack([g_lo, g_hi, g_hi, g_lo], axis=1), (1, 1, LANES // HALF_DIM))
    return g[:, :, None, :] * rope[None]


def _weight_prep_kernel(w_ref, perm_ref, *out_refs, n_rot):
    for t in range(n_rot // MXU_TILE):
        cs = slice(t * MXU_TILE, (t + 1) * MXU_TILE)
        out_refs[0][:, cs] = jnp.dot(w_ref[:, cs].astype(BF16), perm_ref[...],
                                     preferred_element_type=F32).astype(BF16)
    if len(out_refs) > 1:
        out_refs[1][...] = w_ref[:, n_rot:].astype(BF16)


def _prepare_weights(w, n_rot, rows_per_step=512):
    layers, rows, cols = w.shape
    new = jnp.arange(MXU_TILE)
    old = ((new % LANES) // HALF_DIM) * HEAD_DIM + (new // LANES) * HALF_DIM + new % HALF_DIM
    perm = (jnp.arange(MXU_TILE)[:, None] == old[None, :]).astype(BF16)
    widths = (n_rot,) + ((cols - n_rot,) if cols > n_rot else ())
    outs = pl.pallas_call(
        functools.partial(_weight_prep_kernel, n_rot=n_rot),
        grid=(layers, rows // rows_per_step),
        in_specs=[pl.BlockSpec((None, rows_per_step, cols), lambda l, i: (l, i, 0)),
                  pl.BlockSpec((MXU_TILE, MXU_TILE), lambda l, i: (0, 0))],
        out_specs=[pl.BlockSpec((None, rows_per_step, wd), lambda l, i: (l, i, 0))
                   for wd in widths],
        out_shape=[jax.ShapeDtypeStruct((layers, rows, wd), BF16) for wd in widths],
        compiler_params=_params(2),
        name="weight_prep",
    )(w, perm)
    return tuple(outs)


def _head_mean_matrix():
    i = jnp.arange(MXU_TILE)
    same = i[:, None] // HALF_DIM == i[None, :] // HALF_DIM
    return (same.astype(F32) / HEAD_DIM).astype(BF16)


def _head_expand_matrix():
    r = jnp.arange(2 * LANES)[:, None] % LANES
    c = jnp.arange(D_MODEL)[None, :] // HEAD_DIM
    return (r == c).astype(BF16)


def kernel(x, a_norm, a_w_qkv, a_q_gain, a_k_gain, a_lam_q1, a_lam_k1, a_lam_q2, a_lam_k2, a_sub_gain, a_w_o, kv_norm, kv_w, kv_k_gain, b_norm, b_w_q, b_q_gain, b_w_o, m_norm, m_w_up, m_w_down):
    batch, seq, _ = x.shape
    n = batch * seq
    scale = HEAD_DIM ** -0.5
    rope = _rope_tables(seq)
    gmat = _head_mean_matrix()
    expand = _head_expand_matrix()
    x2 = x.reshape(n, D_MODEL)
    gw = N_GROUPS * D_MODEL
    dil = tuple(r for _, r in DIL_GROUPS)

    a_w = _prepare_weights(a_w_qkv, 2 * D_MODEL)
    kv_weights = _prepare_weights(kv_w[None], gw)
    b_w = _prepare_weights(b_w_q, gw)
    a_wo, b_wo = a_w_o.astype(BF16), b_w_o.astype(BF16)
    w_up, w_down = m_w_up.astype(BF16), m_w_down.astype(BF16)

    k_sh = v_sh = None
    for layer in range(DEPTH):
        if layer < N_A_LAYERS:
            lam_init = 0.8 - 0.6 * math.exp(-0.3 * layer)
            tables = _rotary_gain_tables(
                rope, jnp.stack([a_q_gain[layer] * (scale * LOG2_E), a_k_gain[layer]]))
            q, k, v = _project(x2, a_norm[layer], a_w, layer, tables, gmat,
                               (HEAD_BLOCKED,) * 3, seq)
            att = _diff_attention(
                q, k, v,
                (a_lam_q1[layer], a_lam_k1[layer], a_lam_q2[layer], a_lam_k2[layer]),
                a_sub_gain[layer], lam_init)
            x2 = _attn_out_mlp(x2, att, a_wo, layer, m_norm[layer], w_up, w_down, layer)
        else:
            if layer == N_A_LAYERS:
                kv = _project(x2, kv_norm, kv_weights, 0, _rotary_gain_tables(rope, kv_k_gain),
                              gmat, dil + dil, seq)
                k_sh, v_sh = kv[:N_GROUPS], kv[N_GROUPS:]
            bl = layer - N_A_LAYERS
            tables = _rotary_gain_tables(rope, b_q_gain[bl] * (scale * LOG2_E))
            qs = _project(x2, b_norm[bl], b_w, bl, tables, gmat, dil, seq)
            stats = [_dilated_group(qs[g], k_sh[g], v_sh[g], g, batch, seq)
                     for g in range(N_GROUPS)]
            outs, maxes, dens = zip(*stats)
            x2 = _mix_out_mlp(x2, outs, maxes, dens, expand, b_wo, bl, m_norm[layer],
                              w_up, w_down, layer)
    return x2.reshape(batch, seq, D_MODEL)
```

```python
import functools
import math

import jax
import jax.numpy as jnp
from jax import lax
from jax.experimental import pallas as pl
from jax.experimental.pallas import tpu as pltpu

D_MODEL = 1024
HEAD_DIM = 64
HALF_DIM = HEAD_DIM // 2
DEPTH = 4
N_A_LAYERS = DEPTH // 2
DIFF_HEADS = D_MODEL // (2 * HEAD_DIM)
DIL_GROUPS = ((128, 1), (512, 4), (2048, 16))
N_GROUPS = len(DIL_GROUPS)
DIL_HEADS = D_MODEL // HEAD_DIM
D_FF = 4 * D_MODEL
ROPE_THETA = 10000.0
BLOCK = 128
EPS = 1e-6

LANES = 128
MXU_TILE = 256
HEADS_PER_TILE = MXU_TILE // HEAD_DIM
HEAD_BLOCKED = 0
SINGLE_OP_STRIDE = 4
VMEM_LIMIT = 56 * 1024 * 1024
NEG_BIG = -1e30
LOG2_E = math.log2(math.e)

F32 = jnp.float32
BF16 = jnp.bfloat16


def _params(n_axes):
    return pltpu.CompilerParams(
        dimension_semantics=("arbitrary",) * n_axes,
        vmem_limit_bytes=VMEM_LIMIT)


def _head_of_lane(shape, axis):
    lane = lax.broadcasted_iota(jnp.int32, shape, axis)
    return (lane % LANES) // HALF_DIM


def _proj_kernel(x_ref, g_ref, rot_ref, gm_ref, *refs, n_rope, n_weights, dilations, tm):
    w_refs, refs = refs[:n_weights], refs[n_weights:]
    out_refs = refs[:len(dilations)]
    slab_scr, part_scr, h_scr = refs[len(dilations):]
    x = x_ref[...]
    ms = jnp.mean(x * x, axis=-1, keepdims=True)
    h_scr[...] = (x * lax.rsqrt(ms + EPS) * g_ref[...]).astype(BF16)
    chunk = 2 * MXU_TILE
    n_chunks = len(dilations) * D_MODEL // chunk

    def matmul(i):
        col = i * chunk
        w_ref = w_refs[0]
        if col >= w_ref.shape[1]:
            w_ref, col = w_refs[1], col - w_ref.shape[1]
        return jnp.dot(h_scr[...], w_ref[:, col:col + chunk],
                       preferred_element_type=F32)

    def epilogue(i, y):
        col = i * chunk
        out_ref, r = out_refs[col // D_MODEL], dilations[col // D_MODEL]
        slabs = [y[:, s * LANES:(s + 1) * LANES] for s in range(chunk // LANES)]
        if col < n_rope:
            sq = jnp.concatenate([slabs[0] * slabs[0] + slabs[1] * slabs[1],
                                  slabs[2] * slabs[2] + slabs[3] * slabs[3]], axis=1)
            ss = jnp.dot(sq.astype(BF16), gm_ref[...], preferred_element_type=F32)
            rs = lax.rsqrt(ss + EPS)
            sec = col // D_MODEL
            rot = []
            for t in range(2):
                rs_t = rs[:, t * LANES:(t + 1) * LANES]
                y0, y1 = slabs[2 * t], slabs[2 * t + 1]
                rot += [(y0 * rot_ref[sec, 0] - y1 * rot_ref[sec, 1]) * rs_t,
                        (y1 * rot_ref[sec, 2] + y0 * rot_ref[sec, 3]) * rs_t]
            slabs = rot
        for s, slab_val in enumerate(slabs):
            lo = col % D_MODEL + s * LANES
            if r == HEAD_BLOCKED:
                width = out_ref.shape[-1]
                out_ref[0, lo // width, :, lo % width:lo % width + LANES] = (
                    slab_val.astype(out_ref.dtype))
            elif r == 1:
                out_ref[:, lo:lo + LANES] = slab_val.astype(out_ref.dtype)
            else:
                slab = (i * (chunk // LANES) + s) % slab_scr.shape[0]
                slab_scr[slab] = slab_val
                if r <= SINGLE_OP_STRIDE:
                    for cls in range(r):
                        out_ref[:, cls * D_MODEL + lo:cls * D_MODEL + lo + LANES] = (
                            slab_scr[slab, pl.ds(cls, tm // r, stride=r), :]
                            .astype(out_ref.dtype))
                else:
                    r1 = SINGLE_OP_STRIDE
                    r2 = r // r1
                    rows1 = tm // r1
                    for c1 in range(r1):
                        part_scr[slab, c1 * rows1:(c1 + 1) * rows1, :] = (
                            slab_scr[slab, pl.ds(c1, rows1, stride=r1), :])
                    for c1 in range(r1):
                        for c2 in range(r2):
                            cls = c1 + r1 * c2
                            out_ref[:, cls * D_MODEL + lo:cls * D_MODEL + lo + LANES] = (
                                part_scr[slab, pl.ds(c1 * rows1 + c2, tm // r, stride=r2), :]
                                .astype(out_ref.dtype))

    for i in range(n_chunks):
        epilogue(i, matmul(i))


def _layer_block(stack, layer):
    return pl.BlockSpec((None,) + stack.shape[1:], lambda i: (layer, 0, 0),
                        pipeline_mode=pl.Buffered(1))


SPILL_ALLOWANCE = 6 * 1024 * 1024


def _proj_row_tile(n_sections, n_rotary):
    for tm in (1024, 512):
        per_row = (2 * D_MODEL * 4
                   + 2 * n_sections * D_MODEL * 2
                   + 16 * LANES * 4 + D_MODEL * 2
                   + 2 * n_rotary * 4 * LANES * 4)
        resident = n_sections * D_MODEL * D_MODEL * 2
        if tm * per_row + resident + SPILL_ALLOWANCE <= VMEM_LIMIT:
            return tm
    raise ValueError("projection does not fit VMEM")


def _project(x2, gain, weights, layer, rot_tables, gmat, dilations, seq):
    n = x2.shape[0]
    n_sec = rot_tables.shape[0]
    assert weights[0].shape[2] == n_sec * D_MODEL
    tm = _proj_row_tile(len(dilations), n_sec)
    pos_blocks = seq // tm
    kern = functools.partial(_proj_kernel, n_rope=n_sec * D_MODEL, n_weights=len(weights),
                             dilations=dilations, tm=tm)
    out_specs, out_shape = [], []
    for sec, r in enumerate(dilations):
        if r == HEAD_BLOCKED:
            width = MXU_TILE if sec < n_sec else LANES
            out_specs.append(pl.BlockSpec((1, D_MODEL // width, tm, width),
                                          lambda i: (i // pos_blocks, 0, i % pos_blocks, 0)))
            out_shape.append(jax.ShapeDtypeStruct((n // seq, D_MODEL // width, seq, width), BF16))
        else:
            out_specs.append(pl.BlockSpec((tm // r, r * D_MODEL), lambda i: (i, 0)))
            out_shape.append(jax.ShapeDtypeStruct((n // r, r * D_MODEL), BF16))
    return pl.pallas_call(
        kern,
        grid=(n // tm,),
        in_specs=[
            pl.BlockSpec((tm, D_MODEL), lambda i: (i, 0)),
            pl.BlockSpec((1, D_MODEL), lambda i: (0, 0)),
            pl.BlockSpec((n_sec, 4, tm, LANES), lambda i: (0, 0, i % pos_blocks, 0)),
            pl.BlockSpec((MXU_TILE, MXU_TILE), lambda i: (0, 0)),
            *[_layer_block(w, layer) for w in weights],
        ],
        out_specs=out_specs,
        out_shape=out_shape,
        scratch_shapes=[pltpu.VMEM((8, tm, LANES), F32), pltpu.VMEM((8, tm, LANES), F32),
                        pltpu.VMEM((tm, D_MODEL), BF16)],
        compiler_params=_params(1),
        name="proj",
    )(x2, gain.reshape(1, D_MODEL), rot_tables, gmat, *weights)


def _diff_attn_kernel(lq1_ref, lk1_ref, lq2_ref, lk2_ref, sg_ref,
                      q_ref, k_ref, v_ref, o_ref, s_scr, p_scr, linv_scr,
                      *, tq, seq, lam_init, row_chunk):
    lam = (jnp.exp(jnp.sum(lq1_ref[...] * lk1_ref[...], axis=-1, keepdims=True))
           - jnp.exp(jnp.sum(lq2_ref[...] * lk2_ref[...], axis=-1, keepdims=True))
           + lam_init)
    owner = _head_of_lane((tq, MXU_TILE), 1)
    rr = lax.broadcasted_iota(jnp.int32, (row_chunk, LANES), 0)
    cc = lax.broadcasted_iota(jnp.int32, (row_chunk, LANES), 1)
    delta = rr - cc

    for qi in reversed(range(seq // tq)):
        buf = qi % 2
        kc = (qi + 1) * tq
        q = q_ref[qi * tq:(qi + 1) * tq, :]
        zero = jnp.zeros_like(q)
        q4 = jnp.concatenate([jnp.where(owner == j, q, zero)
                              for j in range(HEADS_PER_TILE)], axis=0)
        s_scr[buf, :, :kc] = lax.dot_general(
            q4, k_ref[:kc, :], (((1,), (1,)), ((), ())),
            preferred_element_type=F32)

        for r in range(HEADS_PER_TILE * tq // row_chunk):
            rows = slice(r * row_chunk, (r + 1) * row_chunk)
            row0 = (r * row_chunk) % tq

            def tile(c):
                t = s_scr[buf, rows, c * LANES:(c + 1) * LANES]
                col0 = c * LANES - qi * tq
                if col0 + LANES - 1 > row0:
                    t = jnp.where(delta >= col0 - row0, t, NEG_BIG)
                return t

            tiles = [tile(c) for c in range(kc // LANES)]
            m = tiles[0]
            for t in tiles[1:]:
                m = jnp.maximum(m, t)
            mrow = jnp.max(m, axis=-1, keepdims=True)
            lsum = None
            for c, t in enumerate(tiles):
                p = jnp.exp2(t - mrow)
                lsum = p if lsum is None else lsum + p
                p_scr[buf, rows, c * LANES:(c + 1) * LANES] = p.astype(BF16)
            l = jnp.sum(lsum, axis=-1, keepdims=True)
            linv_scr[buf, rows, :] = jnp.broadcast_to(1.0 / l, (row_chunk, LANES))

        for hh in range(HEADS_PER_TILE // 2):
            rows = slice(hh * 2 * tq, (hh + 1) * 2 * tq)
            acc = jnp.dot(p_scr[buf, rows, :kc], v_ref[hh, :kc, :],
                          preferred_element_type=F32)
            acc = acc * linv_scr[buf, rows]
            o = acc[:tq] - lam * acc[tq:]
            ms = jnp.mean(o * o, axis=-1, keepdims=True)
            o = o * lax.rsqrt(ms + EPS) * sg_ref[...] * (1.0 - lam_init)
            o_ref[hh, qi * tq:(qi + 1) * tq, :] = o.astype(o_ref.dtype)


def _diff_attention(q, k, v, lam_vecs, sub_gain, lam_init, tq=256, row_chunk=64):
    batch, heads, seq, _ = v.shape
    pair = HEADS_PER_TILE // 2
    rows = HEADS_PER_TILE * tq
    vec = pl.BlockSpec((1, HEAD_DIM), lambda b, t: (0, 0))
    qk_spec = pl.BlockSpec((None, None, seq, MXU_TILE), lambda b, t: (b, t, 0, 0))
    head = pl.BlockSpec((None, pair, seq, LANES), lambda b, t: (b, t, 0, 0))
    kern = functools.partial(_diff_attn_kernel, tq=tq, seq=seq, lam_init=lam_init,
                             row_chunk=row_chunk)
    return pl.pallas_call(
        kern,
        grid=(batch, heads // pair),
        in_specs=[vec, vec, vec, vec,
                  pl.BlockSpec((1, 2 * HEAD_DIM), lambda b, t: (0, 0)),
                  qk_spec, qk_spec, head],
        out_specs=head,
        out_shape=jax.ShapeDtypeStruct(v.shape, BF16),
        scratch_shapes=[pltpu.VMEM((2, rows, seq), F32),
                        pltpu.VMEM((2, rows, seq), BF16),
                        pltpu.VMEM((2, rows, LANES), F32)],
        compiler_params=_params(2),
        name="diff_attn",
    )(*[u.reshape(1, HEAD_DIM) for u in lam_vecs], sub_gain.reshape(1, 2 * HEAD_DIM),
      q, k, v)


def _dilated_kernel(q_ref, k_ref, v_ref, o_ref, max_ref, den_ref, *, tl, has_prev, classes):
    li = pl.program_id(2)
    nqb = tl // BLOCK
    nkeys = 2 * BLOCK if has_prev else BLOCK
    owner = _head_of_lane((BLOCK, MXU_TILE), 1)
    vhead = lax.broadcasted_iota(jnp.int32, (BLOCK, MXU_TILE), 1) // HEAD_DIM
    row = lax.broadcasted_iota(jnp.int32, (BLOCK, nkeys), 0)
    colm = lax.broadcasted_iota(jnp.int32, (BLOCK, nkeys), 1)
    delta = row - colm

    for qb in range(nqb):
        gb = li * nqb + qb
        rows_q = slice(qb * BLOCK, (qb + 1) * BLOCK)
        if has_prev:
            first = jnp.maximum(gb - 1, 0)
            k0 = pl.multiple_of(first * BLOCK, BLOCK)
            dist = delta + (gb - first) * BLOCK
            valid = (dist >= 0) & (dist <= BLOCK)
        else:
            k0 = pl.multiple_of(gb * BLOCK, BLOCK)
            valid = delta >= 0
        bias = jnp.where(valid, 0.0, NEG_BIG)
        for cl in range(classes):
            stat_cols = slice(cl * LANES, (cl + 1) * LANES)
            max_ref[0, rows_q, stat_cols] = jnp.zeros((BLOCK, LANES), F32)
            den_ref[0, rows_q, stat_cols] = jnp.ones((BLOCK, LANES), F32)
            for t in range(D_MODEL // MXU_TILE):
                lo = cl * D_MODEL + t * MXU_TILE
                cs = slice(lo, lo + MXU_TILE)
                q = q_ref[0, rows_q, cs]
                kb = k_ref[0, pl.ds(k0, nkeys), cs]
                vb = v_ref[0, pl.ds(k0, nkeys), cs]
                zero = jnp.zeros_like(q)
                q4 = jnp.concatenate([jnp.where(owner == j, q, zero)
                                      for j in range(HEADS_PER_TILE)], axis=0)
                s = lax.dot_general(q4, kb, (((1,), (1,)), ((), ())),
                                    preferred_element_type=F32)
                s = (s.reshape(HEADS_PER_TILE, BLOCK, nkeys) + bias[None]
                     ).reshape(HEADS_PER_TILE * BLOCK, nkeys)
                m = jnp.max(s, axis=-1, keepdims=True)
                p = jnp.exp2(s - m)
                den = jnp.sum(p, axis=-1, keepdims=True)
                pv = jnp.dot(p.astype(BF16), vb, preferred_element_type=F32)
                o = pv[:BLOCK]
                for j in range(HEADS_PER_TILE):
                    rows = slice(j * BLOCK, (j + 1) * BLOCK)
                    col = cl * LANES + HEADS_PER_TILE * t + j
                    if j:
                        o = jnp.where(vhead == j, pv[rows], o)
                    max_ref[0, rows_q, col:col + 1] = m[rows]
                    den_ref[0, rows_q, col:col + 1] = den[rows]
                o_ref[0, rows_q, cs] = o.astype(o_ref.dtype)


DILATED_ROWS_PER_STEP = 1024


def _dilated_group(q_g, k_g, v_g, g, batch, seq):
    _, r = DIL_GROUPS[g]
    sub_len = seq // r
    tl = min(sub_len, DILATED_ROWS_PER_STEP)
    classes = min(r, DILATED_ROWS_PER_STEP // tl)
    view = (batch, sub_len, r * D_MODEL)
    kern = functools.partial(_dilated_kernel, tl=tl, has_prev=sub_len > BLOCK,
                             classes=classes)
    width = classes * D_MODEL
    stat_spec = pl.BlockSpec((1, tl, classes * LANES), lambda b, c, i: (b, i, c))
    stat_shape = jax.ShapeDtypeStruct((batch, sub_len, r * LANES), F32)
    o, mx, den = pl.pallas_call(
        kern,
        grid=(batch, r // classes, sub_len // tl),
        in_specs=[
            pl.BlockSpec((1, tl, width), lambda b, c, i: (b, i, c)),
            pl.BlockSpec((1, sub_len, width), lambda b, c, i: (b, 0, c)),
            pl.BlockSpec((1, sub_len, width), lambda b, c, i: (b, 0, c)),
        ],
        out_specs=[pl.BlockSpec((1, tl, width), lambda b, c, i: (b, i, c)),
                   stat_spec, stat_spec],
        out_shape=[jax.ShapeDtypeStruct((batch, sub_len, r * D_MODEL), BF16),
                   stat_shape, stat_shape],
        compiler_params=_params(3),
        name=f"dilated_g{g}",
    )(q_g.reshape(view), k_g.reshape(view), v_g.reshape(view))
    n = batch * seq
    return (o.reshape(n // r, r * D_MODEL), mx.reshape(n // r, r * LANES),
            den.reshape(n // r, r * LANES))


FF_CHUNK = 512


def _mlp_into(out_ref, x, g_ref, wu_ref, wd_ref):
    ms = jnp.mean(x * x, axis=-1, keepdims=True)
    h = (x * lax.rsqrt(ms + EPS) * g_ref[...]).astype(BF16)
    out_ref[...] = x
    for c in range(D_FF // FF_CHUNK):
        cs = slice(c * FF_CHUNK, (c + 1) * FF_CHUNK)
        u = jnp.dot(h, wu_ref[:, cs], preferred_element_type=F32)
        a = jnp.square(jnp.maximum(u, 0.0)).astype(BF16)
        out_ref[...] += jnp.dot(a, wd_ref[cs, :], preferred_element_type=F32)


def _attn_out_mlp_kernel(x_ref, att_ref, wo_ref, g_ref, wu_ref, wd_ref, out_ref):
    att = jnp.concatenate([att_ref[h] for h in range(att_ref.shape[0])], axis=1)
    x1 = x_ref[...] + jnp.dot(att, wo_ref[...], preferred_element_type=F32)
    _mlp_into(out_ref, x1, g_ref, wu_ref, wd_ref)


def _mix_out_mlp_kernel(x_ref, o0_ref, o1_ref, o2_ref, m0_ref, m1_ref, m2_ref,
                        d0_ref, d1_ref, d2_ref, e_ref, wo_ref, g_ref, wu_ref, wd_ref,
                        out_ref, o_scr, st_scr, *, tm):
    n_slabs = D_MODEL // LANES
    for g, (o_ref, m_ref, d_ref) in enumerate(((o1_ref, m1_ref, d1_ref),
                                               (o2_ref, m2_ref, d2_ref))):
        r = DIL_GROUPS[g + 1][1]
        for cls in range(r):
            dst = pl.ds(cls, tm // r, stride=r)
            st_scr[0, g, dst, :] = m_ref[:, cls * LANES:(cls + 1) * LANES]
            st_scr[1, g, dst, :] = d_ref[:, cls * LANES:(cls + 1) * LANES]
            for s in range(n_slabs):
                lo = cls * D_MODEL + s * LANES
                o_scr[g, s, dst, :] = o_ref[:, lo:lo + LANES].astype(F32)

    maxes = (m0_ref[...], st_scr[0, 0], st_scr[0, 1])
    dens = (d0_ref[...], st_scr[1, 0], st_scr[1, 1])
    top = jnp.maximum(jnp.maximum(maxes[0], maxes[1]), maxes[2])
    es = [jnp.exp2(m - top) for m in maxes]
    inv = 1.0 / (es[0] * dens[0] + es[1] * dens[1] + es[2] * dens[2])
    wexp = []
    for e in es:
        wg = e * inv
        hi = wg.astype(BF16)
        lo = (wg - hi.astype(F32)).astype(BF16)
        wexp.append(jnp.dot(jnp.concatenate([hi, lo], axis=1), e_ref[...],
                            preferred_element_type=F32))
    slabs = []
    for s in range(n_slabs):
        cs = slice(s * LANES, (s + 1) * LANES)
        mixed = (wexp[0][:, cs] * o0_ref[:, cs].astype(F32)
                 + wexp[1][:, cs] * o_scr[0, s] + wexp[2][:, cs] * o_scr[1, s])
        slabs.append(mixed.astype(BF16))
    x1 = x_ref[...] + jnp.dot(jnp.concatenate(slabs, axis=1), wo_ref[...],
                              preferred_element_type=F32)
    _mlp_into(out_ref, x1, g_ref, wu_ref, wd_ref)


def _resident(shape):
    return pl.BlockSpec(shape, lambda i: (0,) * len(shape), pipeline_mode=pl.Buffered(1))


def _attn_out_mlp(x2, att, w_o, attn_layer, gain, w_up, w_down, layer, tm=1024):
    n = x2.shape[0]
    _, heads, seq, width = att.shape
    pos_blocks = seq // tm
    row = pl.BlockSpec((tm, D_MODEL), lambda i: (i, 0))
    att_spec = pl.BlockSpec((None, heads, tm, width),
                            lambda i: (i // pos_blocks, 0, i % pos_blocks, 0))
    return pl.pallas_call(
        _attn_out_mlp_kernel,
        grid=(n // tm,),
        in_specs=[row, att_spec, _layer_block(w_o, attn_layer), _resident((1, D_MODEL)),
                  _layer_block(w_up, layer), _layer_block(w_down, layer)],
        out_specs=row,
        out_shape=jax.ShapeDtypeStruct((n, D_MODEL), F32),
        compiler_params=_params(1),
        name="attn_out_mlp",
    )(x2, att, w_o, gain.reshape(1, D_MODEL), w_up, w_down)


def _mix_out_mlp(x2, outs, maxes, dens, expand, w_o, attn_layer, gain, w_up, w_down, layer,
                 tm=512):
    n = x2.shape[0]
    row = pl.BlockSpec((tm, D_MODEL), lambda i: (i, 0))
    o_specs = [pl.BlockSpec((tm // r, r * D_MODEL), lambda i: (i, 0)) for _, r in DIL_GROUPS]
    st_specs = [pl.BlockSpec((tm // r, r * LANES), lambda i: (i, 0)) for _, r in DIL_GROUPS]
    return pl.pallas_call(
        functools.partial(_mix_out_mlp_kernel, tm=tm),
        grid=(n // tm,),
        in_specs=[row, *o_specs, *st_specs, *st_specs,
                  _resident((2 * LANES, D_MODEL)), _layer_block(w_o, attn_layer),
                  _resident((1, D_MODEL)), _layer_block(w_up, layer),
                  _layer_block(w_down, layer)],
        out_specs=row,
        out_shape=jax.ShapeDtypeStruct((n, D_MODEL), F32),
        scratch_shapes=[pltpu.VMEM((N_GROUPS - 1, D_MODEL // LANES, tm, LANES), F32),
                        pltpu.VMEM((2, N_GROUPS - 1, tm, LANES), F32)],
        compiler_params=_params(1),
        name="mix_out_mlp",
    )(x2, *outs, *maxes, *dens, expand, w_o, gain.reshape(1, D_MODEL), w_up, w_down)


def _rope_tables(seq):
    inv = 1.0 / (ROPE_THETA ** (jnp.arange(0, HEAD_DIM, 2, dtype=F32) / HEAD_DIM))
    ang = jnp.arange(seq, dtype=F32)[:, None] * inv[None, :]
    planes = jnp.stack([jnp.cos(ang), jnp.sin(ang), jnp.cos(ang), jnp.sin(ang)])
    return jnp.tile(planes, (1, 1, LANES // HALF_DIM))


def _rotary_gain_tables(rope, gains):
    g_lo, g_hi = gains[:, :HALF_DIM], gains[:, HALF_DIM:]
    g = jnp.tile(jnp.stack([g_lo, g_hi, g_hi, g_lo], axis=1), (1, 1, LANES // HALF_DIM))
    return g[:, :, None, :] * rope[None]


def _weight_prep_kernel(w_ref, perm_ref, *out_refs, n_rot):
    for t in range(n_rot // MXU_TILE):
        cs = slice(t * MXU_TILE, (t + 1) * MXU_TILE)
        out_refs[0][:, cs] = jnp.dot(w_ref[:, cs].astype(BF16), perm_ref[...],
                                     preferred_element_type=F32).astype(BF16)
    if len(out_refs) > 1:
        out_refs[1][...] = w_ref[:, n_rot:].astype(BF16)


def _prepare_weights(w, n_rot, rows_per_step=512):
    layers, rows, cols = w.shape
    new = jnp.arange(MXU_TILE)
    old = ((new % LANES) // HALF_DIM) * HEAD_DIM + (new // LANES) * HALF_DIM + new % HALF_DIM
    perm = (jnp.arange(MXU_TILE)[:, None] == old[None, :]).astype(BF16)
    widths = (n_rot,) + ((cols - n_rot,) if cols > n_rot else ())
    outs = pl.pallas_call(
        functools.partial(_weight_prep_kernel, n_rot=n_rot),
        grid=(layers, rows // rows_per_step),
        in_specs=[pl.BlockSpec((None, rows_per_step, cols), lambda l, i: (l, i, 0)),
                  pl.BlockSpec((MXU_TILE, MXU_TILE), lambda l, i: (0, 0))],
        out_specs=[pl.BlockSpec((None, rows_per_step, wd), lambda l, i: (l, i, 0))
                   for wd in widths],
        out_shape=[jax.ShapeDtypeStruct((layers, rows, wd), BF16) for wd in widths],
        compiler_params=_params(2),
        name="weight_prep",
    )(w, perm)
    return tuple(outs)


def _head_mean_matrix():
    i = jnp.arange(MXU_TILE)
    same = i[:, None] // HALF_DIM == i[None, :] // HALF_DIM
    return (same.astype(F32) / HEAD_DIM).astype(BF16)


def _head_expand_matrix():
    r = jnp.arange(2 * LANES)[:, None] % LANES
    c = jnp.arange(D_MODEL)[None, :] // HEAD_DIM
    return (r == c).astype(BF16)


def kernel(x, a_norm, a_w_qkv, a_q_gain, a_k_gain, a_lam_q1, a_lam_k1, a_lam_q2, a_lam_k2, a_sub_gain, a_w_o, kv_norm, kv_w, kv_k_gain, b_norm, b_w_q, b_q_gain, b_w_o, m_norm, m_w_up, m_w_down):
    batch, seq, _ = x.shape
    n = batch * seq
    scale = HEAD_DIM ** -0.5
    rope = _rope_tables(seq)
    gmat = _head_mean_matrix()
    expand = _head_expand_matrix()
    x2 = x.reshape(n, D_MODEL)
    gw = N_GROUPS * D_MODEL
    dil = tuple(r for _, r in DIL_GROUPS)

    a_w = _prepare_weights(a_w_qkv, 2 * D_MODEL)
    kv_weights = _prepare_weights(kv_w[None], gw)
    b_w = _prepare_weights(b_w_q, gw)
    a_wo, b_wo = a_w_o.astype(BF16), b_w_o.astype(BF16)
    w_up, w_down = m_w_up.astype(BF16), m_w_down.astype(BF16)

    k_sh = v_sh = None
    for layer in range(DEPTH):
        if layer < N_A_LAYERS:
            lam_init = 0.8 - 0.6 * math.exp(-0.3 * layer)
            tables = _rotary_gain_tables(
                rope, jnp.stack([a_q_gain[layer] * (scale * LOG2_E), a_k_gain[layer]]))
            q, k, v = _project(x2, a_norm[layer], a_w, layer, tables, gmat,
                               (HEAD_BLOCKED,) * 3, seq)
            att = _diff_attention(
                q, k, v,
                (a_lam_q1[layer], a_lam_k1[layer], a_lam_q2[layer], a_lam_k2[layer]),
                a_sub_gain[layer], lam_init)
            x2 = _attn_out_mlp(x2, att, a_wo, layer, m_norm[layer], w_up, w_down, layer)
        else:
            if layer == N_A_LAYERS:
                kv = _project(x2, kv_norm, kv_weights, 0, _rotary_gain_tables(rope, kv_k_gain),
                              gmat, dil + dil, seq)
                k_sh, v_sh = kv[:N_GROUPS], kv[N_GROUPS:]
            bl = layer - N_A_LAYERS
            tables = _rotary_gain_tables(rope, b_q_gain[bl] * (scale * LOG2_E))
            qs = _project(x2, b_norm[bl], b_w, bl, tables, gmat, dil, seq)
            stats = [_dilated_group(qs[g], k_sh[g], v_sh[g], g, batch, seq)
                     for g in range(N_GROUPS)]
            outs, maxes, dens = zip(*stats)
            x2 = _mix_out_mlp(x2, outs, maxes, dens, expand, b_wo, bl, m_norm[layer],
                              w_up, w_down, layer)
    return x2.reshape(batch, seq, D_MODEL)
```
